```python
import math
import numpy as np
import jax
import jax.numpy as jnp
from jax import lax

D_MODEL = 2048
BATCH = 16
SEQ = 256
DEPTH = 4
DEC_BATCH = 4
DEC_SEQ = 1024
PAST_LEN = 256

GRID_W = 64
N_MIXERS = 4
MIX_W = D_MODEL // N_MIXERS
HEAD_DIM = 128
NA_HEADS = MIX_W // HEAD_DIM
SGU_GROUPS = MIX_W // HEAD_DIM
HGRN_HEADS = MIX_W // HEAD_DIM
HGRN_DK = HEAD_DIM
HGRN_DV = HEAD_DIM
POOL_WINDOWS = (2, 4, 8, 16)
POOL_GROUPS = len(POOL_WINDOWS)
POOL_GW = MIX_W // POOL_GROUPS
N_PROJ = 11
NA_WIN_ROWS = 8
NA_WIN_COLS = 16
NA_COL_BLOCK = 16
NA_COL_BAND = 32
NA_COL_BLOCKS = GRID_W // NA_COL_BLOCK
CTX_Q_BLOCK = 128
SGU_CHUNK = 128
HGRN_CHUNK = 16
N_EXPERTS = 32
TOP_K = 4
D_FF = D_MODEL
SWIGLU_ALPHA = 1.702
SWIGLU_LIMIT = 7.0
MOE_BLOCK = 128
N_MOD = 6
EPS = 1e-6
LB_TINY = 1e-30
NEG_BIG = -1e30
ATTN_SCALE = HEAD_DIM ** -0.5
F32 = jnp.float32

kernel_name = 'hybrid_prefix_diffusion_step'


def rms_norm(x, g):
    xf = x.astype(F32)
    y = xf * lax.rsqrt(jnp.mean(xf * xf, axis=-1, keepdims=True) + EPS)
    return (y * g.astype(F32)).astype(x.dtype)


def adaln(cond, w, b):
    m = jax.nn.silu(cond) @ w + b
    return jnp.split(m[:, None, :], N_MOD, axis=-1)


def context_attention(q, k, v):
    B, S, H, hd = q.shape
    qb = jnp.moveaxis(q.reshape(B, S // CTX_Q_BLOCK, CTX_Q_BLOCK, H, hd), 1, 0)

    def attend(qi):
        s = jnp.einsum('bqhd,bkhd->bhqk', qi, k).astype(F32) * ATTN_SCALE
        p = jax.nn.softmax(s, axis=-1).astype(v.dtype)
        return jnp.einsum('bhqk,bkhd->bqhd', p, v)

    o = lax.map(attend, qb)
    return jnp.moveaxis(o, 0, 1).reshape(B, S, H, hd)


def _na_index(rows):
    wr = min(NA_WIN_ROWS, rows)
    r = np.arange(rows)
    row_idx = np.clip(r - wr // 2, 0, rows - wr)[:, None] + np.arange(wr)[None, :]
    m = np.arange(NA_COL_BLOCKS)
    band_idx = (np.clip(m * NA_COL_BLOCK - NA_WIN_COLS // 2, 0, GRID_W - NA_COL_BAND)[:, None]
                + np.arange(NA_COL_BAND)[None, :])
    qcol = m[:, None] * NA_COL_BLOCK + np.arange(NA_COL_BLOCK)[None, :]
    col_start = np.clip(qcol - NA_WIN_COLS // 2, 0, GRID_W - NA_WIN_COLS)
    rel = band_idx[:, None, :] - col_start[:, :, None]
    col_mask = (rel >= 0) & (rel < NA_WIN_COLS)
    d_row = row_idx - r[:, None] + NA_WIN_ROWS - 1
    d_col = np.clip(band_idx[:, None, :] - qcol[:, :, None] + NA_WIN_COLS - 1, 0, 2 * NA_WIN_COLS - 2)
    return wr, row_idx, band_idx, col_mask, d_row, d_col


def neighbourhood_attention(q, k, v, ctx_k, ctx_v, rpb):
    B, L, H, hd = q.shape
    rows = L // GRID_W
    wr, row_idx, band_idx, col_mask, d_row, d_col = _na_index(rows)
    qb = q.reshape(B, rows, NA_COL_BLOCKS, NA_COL_BLOCK, H, hd)

    def band(a):
        a = jnp.take(a.reshape(B, rows, GRID_W, H, hd), row_idx, axis=1)
        return jnp.take(a, band_idx, axis=3)

    kb, vb = band(k), band(v)
    bias = jnp.transpose(rpb, (1, 2, 0))[d_row[:, None, None, :, None], d_col[None, :, :, None, :]]
    bias = jnp.transpose(bias, (0, 1, 2, 5, 3, 4)).astype(F32)
    s_loc = jnp.einsum('brmqhd,brimjhd->brmqhij', qb, kb).astype(F32) * ATTN_SCALE + bias
    s_loc = jnp.where(col_mask[None, None, :, :, None, None, :], s_loc, NEG_BIG)
    s_ctx = jnp.einsum('brmqhd,bshd->brmqhs', qb, ctx_k).astype(F32) * ATTN_SCALE
    n_loc = wr * NA_COL_BAND
    s = jnp.concatenate([s_loc.reshape(s_loc.shape[:5] + (n_loc,)), s_ctx], axis=-1)
    p = jax.nn.softmax(s, axis=-1).astype(v.dtype)
    p_loc = p[..., :n_loc].reshape(s_loc.shape)
    o = (jnp.einsum('brmqhij,brimjhd->brmqhd', p_loc, vb)
         + jnp.einsum('brmqhs,bshd->brmqhd', p[..., n_loc:], ctx_v))
    return o.reshape(B, L, H, hd)


def spatial_gating(u, v, w_s, b_s):
    B, L, _ = u.shape
    n = L // SGU_CHUNK
    u = jax.nn.gelu(u)
    vf = jax.nn.gelu(v).reshape(B, n, SGU_CHUNK, SGU_GROUPS, HEAD_DIM).astype(F32)
    vn = (vf * lax.rsqrt(jnp.mean(vf * vf, axis=-1, keepdims=True) + EPS)).astype(v.dtype)
    mixed = jnp.einsum('gts,bnsgc->bntgc', w_s, vn) + b_s.T[None, None, :, :, None]
    return u * mixed.reshape(B, L, MIX_W)


def multiscale_pool(x, w_pool, scale):
    B, L, _ = x.shape
    xg = x.reshape(B, L, POOL_GROUPS, POOL_GW)
    cs = jnp.pad(jnp.cumsum(xg.astype(F32), axis=1), ((0, 0), (1, 0), (0, 0), (0, 0)))
    t = np.arange(L)
    pooled = []
    for g, w in enumerate(POOL_WINDOWS):
        lo = np.clip(t - w // 2, 0, L)
        hi = np.clip(t + w // 2, 0, L)
        cnt = jnp.asarray((hi - lo).astype(np.float32))[None, :, None]
        csg = cs[:, :, g]
        pooled.append((csg[:, hi] - csg[:, lo]) / cnt - xg[:, :, g].astype(F32))
    pooled = jnp.stack(pooled, axis=2).astype(x.dtype)
    y = jnp.einsum('blgc,gcd->blgd', pooled, w_pool).reshape(B, L, MIX_W)
    return y * scale


def log_forget(z, lb):
    return jnp.logaddexp(jnp.log(lb + LB_TINY), jnp.log1p(-lb) + jax.nn.log_sigmoid(z.astype(F32)))


def hgrn2_chunk_scan(q, k, v, logf, s0):
    B, L, H, DK = q.shape
    n, C = L // HGRN_CHUNK, HGRN_CHUNK
    rs = lambda a: a.reshape(B, n, C, H, a.shape[-1])
    q, k, v, logf = rs(q), rs(k), rs(v), rs(logf)
    cum = jnp.cumsum(logf, axis=2)
    tri = np.tril(np.ones((C, C), dtype=bool))[None, None, :, :, None, None]
    diff = jnp.where(tri, cum[:, :, :, None] - cum[:, :, None], 0.0)
    decay = jnp.where(tri, jnp.exp(diff), 0.0)
    att = jnp.einsum('bnthk,bnshk,bntshk->bnhts', q, k, decay)
    o_intra = jnp.einsum('bnhts,bnshv->bnthv', att, v)
    last = cum[:, :, -1]
    upd = jnp.einsum('bnshk,bnshv->bnhkv', k * jnp.exp(last[:, :, None] - cum), v)

    def step(s, inp):
        a, u = inp
        return a[..., None] * s + u, s

    s_final, s_starts = lax.scan(step, s0.astype(F32),
                                 (jnp.moveaxis(jnp.exp(last), 1, 0), jnp.moveaxis(upd, 1, 0)))
    s_starts = jnp.moveaxis(s_starts, 0, 1)
    o_inter = jnp.einsum('bnthk,bnhkv->bnthv', q * jnp.exp(cum), s_starts)
    return (o_intra + o_inter).reshape(B, L, H, v.shape[-1]), s_final


def hgrn2_mixer(q_raw, i_raw, ff_raw, fb_raw, g_raw, lb_f, lb_b, norm_g, s0_f, s0_b):
    B, L, _ = q_raw.shape
    heads = lambda a: a.reshape(B, L, HGRN_HEADS, -1)
    flip = lambda a: jnp.flip(a, axis=1)
    q = heads(jax.nn.silu(q_raw.astype(F32)))
    v = heads(i_raw.astype(F32))
    logf_f = heads(log_forget(ff_raw, lb_f))
    logf_b = heads(log_forget(fb_raw, lb_b))
    o_f, s_f = hgrn2_chunk_scan(q, -jnp.expm1(logf_f), v, logf_f, s0_f)
    o_b, s_b = hgrn2_chunk_scan(flip(q), flip(-jnp.expm1(logf_b)), flip(v), flip(logf_b), s0_b)
    o = o_f + flip(o_b)
    o = o * lax.rsqrt(jnp.mean(o * o, axis=-1, keepdims=True) + EPS)
    o = o.reshape(B, L, MIX_W) * norm_g.astype(F32) * jax.nn.silu(g_raw.astype(F32))
    return o.astype(q_raw.dtype), s_f, s_b


def token_mixers(h, w_in, w_out, rpb, sgu_w, sgu_b, lb_f, lb_b, hgrn_g, pool_w, pool_s,
                 ctx_k=None, ctx_v=None, s0_f=None, s0_b=None):
    B, L, _ = h.shape
    aq, ak, av, bu, bv, cq, ci, cff, cfb, cg, dx = jnp.split(h @ w_in, N_PROJ, axis=-1)
    heads = lambda a: a.reshape(B, L, NA_HEADS, HEAD_DIM)
    q, k, v = heads(aq), heads(ak), heads(av)
    if ctx_k is None:
        o_a = context_attention(q, k, v)
        s0_f = s0_b = jnp.zeros((B, HGRN_HEADS, HGRN_DK, HGRN_DV), F32)
    else:
        o_a = neighbourhood_attention(q, k, v, ctx_k, ctx_v, rpb)
    o_b = spatial_gating(bu, bv, sgu_w, sgu_b)
    o_c, s_f, s_b = hgrn2_mixer(cq, ci, cff, cfb, cg, lb_f, lb_b, hgrn_g, s0_f, s0_b)
    o_d = multiscale_pool(dx, pool_w, pool_s)
    mix = jnp.concatenate([o_a.reshape(B, L, MIX_W), o_b, o_c, o_d], axis=-1) @ w_out
    return mix, k, v, s_f, s_b


def moe_ffn(h, router_w, router_b, w_gu, b_gu, w_dn, b_dn):
    B, L, D = h.shape
    T = B * L
    TK = T * TOP_K
    x = h.reshape(T, D)
    logits = (x @ router_w).astype(F32) + router_b.astype(F32)
    top_val, top_idx = lax.top_k(logits, TOP_K)
    gates = jax.nn.softmax(top_val, axis=-1)
    e_flat = top_idx.reshape(TK).astype(jnp.int32)
    tok_flat = jnp.arange(TK, dtype=jnp.int32) // TOP_K
    order = jnp.argsort(e_flat)
    e_sorted = e_flat[order]
    counts = jnp.bincount(e_flat, length=N_EXPERTS).astype(jnp.int32)
    padded = (counts + MOE_BLOCK - 1) // MOE_BLOCK * MOE_BLOCK
    start = jnp.cumsum(counts) - counts
    pstart = jnp.cumsum(padded) - padded
    dest = pstart[e_sorted] + jnp.arange(TK, dtype=jnp.int32) - start[e_sorted]
    NB = TK // MOE_BLOCK + N_EXPERTS
    slot_tok = jnp.full((NB * MOE_BLOCK,), T, jnp.int32).at[dest].set(tok_flat[order])
    slot_gate = jnp.zeros((NB * MOE_BLOCK,), F32).at[dest].set(gates.reshape(TK)[order])
    block_expert = jnp.minimum(
        jnp.searchsorted(jnp.cumsum(padded), jnp.arange(NB, dtype=jnp.int32) * MOE_BLOCK, side='right'),
        N_EXPERTS - 1)
    x_pad = jnp.concatenate([x, jnp.zeros((1, D), x.dtype)], axis=0)

    def run_block(args):
        e, toks, g = args
        gu = x_pad[toks] @ w_gu[e] + b_gu[e]
        gate = jnp.minimum(gu[:, :D_FF], SWIGLU_LIMIT)
        up = jnp.clip(gu[:, D_FF:], -SWIGLU_LIMIT, SWIGLU_LIMIT)
        act = (up + 1) * gate * jax.nn.sigmoid(gate * SWIGLU_ALPHA)
        y = act @ w_dn[e] + b_dn[e]
        return y * g[:, None].astype(y.dtype)

    ys = lax.map(run_block, (block_expert, slot_tok.reshape(NB, MOE_BLOCK), slot_gate.reshape(NB, MOE_BLOCK)))
    out = jax.ops.segment_sum(ys.reshape(NB * MOE_BLOCK, D), slot_tok, num_segments=T + 1)[:T]
    return out.reshape(B, L, D)


def trunk_layer(x, cond, w_mod, b_mod, norm1_g, norm2_g, w_in, w_out, rpb, sgu_w, sgu_b, lb_f, lb_b,
                hgrn_g, pool_w, pool_s, router_w, router_b, w_gu, b_gu, w_dn, b_dn,
                ctx_k=None, ctx_v=None, s0_f=None, s0_b=None):
    sh1, sc1, g1, sh2, sc2, g2 = adaln(cond, w_mod, b_mod)
    h = rms_norm(x, norm1_g) * (1 + sc1) + sh1
    mix, k, v, s_f, s_b = token_mixers(h, w_in, w_out, rpb, sgu_w, sgu_b, lb_f, lb_b, hgrn_g, pool_w, pool_s,
                                       ctx_k, ctx_v, s0_f, s0_b)
    x = x + g1 * mix
    h = rms_norm(x, norm2_g) * (1 + sc2) + sh2
    x = x + g2 * moe_ffn(h, router_w, router_b, w_gu, b_gu, w_dn, b_dn)
    return x, k, v, s_f, s_b


def _uni(key, shape, std):
    a = std * math.sqrt(3.0)
    return jax.random.uniform(key, shape, jnp.float32, -a, a)


def _gain(key, shape):
    return jax.random.uniform(key, shape, jnp.float32, 0.9, 1.1)


def setup_inputs(seed: int = 0) -> dict:
    key = jax.random.key(seed)
    ks = jax.random.split(key, 27)
    D = D_MODEL
    return {
        'x_prompt': jax.random.normal(ks[0], (BATCH, SEQ, D), jnp.float32),
        'x_sample': jax.random.normal(ks[1], (DEC_BATCH, DEC_SEQ, D), jnp.float32),
        'cache_k': jax.random.normal(ks[2], (DEC_BATCH, DEPTH, PAST_LEN, NA_HEADS, HEAD_DIM), jnp.float32),
        'cache_v': jax.random.normal(ks[3], (DEC_BATCH, DEPTH, PAST_LEN, NA_HEADS, HEAD_DIM), jnp.float32),
        'state_hgrn': _uni(ks[4], (DEC_BATCH, DEPTH, 2, HGRN_HEADS, HGRN_DK, HGRN_DV), 0.3),
        'c': jax.random.normal(ks[5], (DEC_BATCH, D), jnp.float32),
        'c_ctx': jax.random.normal(ks[6], (D,), jnp.float32),
        'w_mod': _uni(ks[7], (DEPTH, D, N_MOD * D), 0.5 * D ** -0.5),
        'b_mod': _uni(ks[8], (DEPTH, N_MOD * D), 0.02),
        'norm1_g': _gain(ks[9], (DEPTH, D)),
        'norm2_g': _gain(ks[10], (DEPTH, D)),
        'w_in': _uni(ks[11], (DEPTH, D, N_PROJ * MIX_W), D ** -0.5),
        'na_rpb': _uni(ks[12], (DEPTH, NA_HEADS, 2 * NA_WIN_ROWS - 1, 2 * NA_WIN_COLS - 1), 0.5),
        'sgu_w': _uni(ks[13], (DEPTH, SGU_GROUPS, SGU_CHUNK, SGU_CHUNK), SGU_CHUNK ** -0.5),
        'sgu_b': _gain(ks[14], (DEPTH, SGU_GROUPS, SGU_CHUNK)),
        'hgrn_lb': jax.random.normal(ks[15], (2, DEPTH, MIX_W), jnp.float32),
        'hgrn_norm_g': _gain(ks[16], (DEPTH, MIX_W)),
        'pool_w': _uni(ks[17], (DEPTH, POOL_GROUPS, POOL_GW, POOL_GW), POOL_GW ** -0.5),
        'pool_scale': _gain(ks[18], (DEPTH, MIX_W)),
        'w_out': _uni(ks[19], (DEPTH, D, D), D ** -0.5),
        'router_w': _uni(ks[20], (DEPTH, D, N_EXPERTS), D ** -0.5),
        'router_b': _uni(ks[21], (DEPTH, N_EXPERTS), 0.01),
        'exp_w_gu': _uni(ks[22], (DEPTH, N_EXPERTS, D, 2 * D_FF), D ** -0.5),
        'exp_b_gu': _uni(ks[23], (DEPTH, N_EXPERTS, 2 * D_FF), 0.01),
        'exp_w_dn': _uni(ks[24], (DEPTH, N_EXPERTS, D_FF, D), D_FF ** -0.5),
        'exp_b_dn': _uni(ks[25], (DEPTH, N_EXPERTS, D), 0.01),
        'final_norm_g': _gain(ks[26], (D,)),
    }


def reference(x_prompt, x_sample, cache_k, cache_v, state_hgrn, c, c_ctx, w_mod, b_mod, norm1_g, norm2_g,
              w_in, na_rpb, sgu_w, sgu_b, hgrn_lb, hgrn_norm_g, pool_w, pool_scale, w_out, router_w,
              router_b, exp_w_gu, exp_b_gu, exp_w_dn, exp_b_dn, final_norm_g):
    lb_p = jax.nn.softmax(hgrn_lb.astype(F32), axis=1)
    lb_all = jnp.maximum(jnp.cumsum(lb_p, axis=1) - lb_p[:, :1], 0.0)
    xp, xs = x_prompt, x_sample
    new_k, new_v, new_s = [], [], []
    for l in range(DEPTH):
        lw = (w_mod[l], b_mod[l], norm1_g[l], norm2_g[l], w_in[l], w_out[l], na_rpb[l], sgu_w[l], sgu_b[l],
              lb_all[0, l], lb_all[1, l], hgrn_norm_g[l], pool_w[l], pool_scale[l], router_w[l], router_b[l],
              exp_w_gu[l], exp_b_gu[l], exp_w_dn[l], exp_b_dn[l])
        xp, k_l, v_l, sf_l, sb_l = trunk_layer(xp, c_ctx[None], *lw)
        new_k.append(k_l)
        new_v.append(v_l)
        new_s.append(jnp.stack([sf_l, sb_l], axis=1).astype(x_prompt.dtype))
        xs, _, _, _, _ = trunk_layer(xs, c, *lw, ctx_k=cache_k[:, l], ctx_v=cache_v[:, l],
                                     s0_f=state_hgrn[:, l, 0], s0_b=state_hgrn[:, l, 1])
    y_prompt = rms_norm(xp, final_norm_g)
    y_sample = rms_norm(xs, final_norm_g)
    return (y_prompt, y_sample, jnp.stack(new_k, axis=1), jnp.stack(new_v, axis=1), jnp.stack(new_s, axis=1))
```

```python
import functools
import math

import numpy as np
import jax
import jax.numpy as jnp
from jax import lax
from jax.experimental import pallas as pl
from jax.experimental.pallas import tpu as pltpu

F32 = jnp.float32
BF16 = jnp.bfloat16

N_MOD = 6
N_PROJ = 11
MIX_W = 512
HEAD_DIM = 128
N_HEADS = 4
GRID_W = 64
NA_WIN_ROWS = 8
NA_WIN_COLS = 16
POOL_WINDOWS = (2, 4, 8, 16)
N_EXPERTS = 32
TOP_K = 4
SWIGLU_ALPHA = 1.702
SWIGLU_LIMIT = 7.0
EPS = 1e-6
LB_TINY = 1e-30
NEG_BIG = -1e30
HGRN_CHUNK = 16

LANES = 128
ROW_TILE = 256
VMEM_LIMIT = 56 * 1024 * 1024


def _params(*sem, vmem=VMEM_LIMIT):
    return pltpu.CompilerParams(dimension_semantics=sem, vmem_limit_bytes=vmem)


def _sigmoid(x):
    return 1.0 / (1.0 + jnp.exp(-x))


def _adaln_kernel(c_ref, w_ref, b_ref, o_ref):
    c = c_ref[...]
    s = (c * _sigmoid(c)).astype(BF16)
    o_ref[...] = jnp.dot(s, w_ref[...].astype(BF16), preferred_element_type=F32) + b_ref[...]


def adaln_all(cond, w_mod, b_mod, tn=1024):
    depth, d, n = w_mod.shape
    rows = cond.shape[0]
    tn = min(tn, n)
    return pl.pallas_call(
        _adaln_kernel,
        grid=(depth, n // tn),
        in_specs=[
            pl.BlockSpec((rows, d), lambda l, j: (0, 0)),
            pl.BlockSpec((None, d, tn), lambda l, j: (l, 0, j)),
            pl.BlockSpec((None, 1, tn), lambda l, j: (l, 0, j)),
        ],
        out_specs=pl.BlockSpec((None, rows, tn), lambda l, j: (l, 0, j)),
        out_shape=jax.ShapeDtypeStruct((depth, rows, n), F32),
        compiler_params=_params("arbitrary", "arbitrary"),
        name="adaln",
    )(cond, w_mod, b_mod.reshape(depth, 1, n))


def _rms(x):
    return x * lax.rsqrt(jnp.mean(x * x, axis=-1, keepdims=True) + EPS)


def _normmod_kernel(x_ref, g_ref, sc_ref, sh_ref, o_ref):
    h = (_rms(x_ref[...]) * g_ref[...]) * (1.0 + sc_ref[...]) + sh_ref[...]
    o_ref[...] = h.astype(o_ref.dtype)


def _mod_spec(d, layer, which, mod_of_tile):
    return pl.BlockSpec((None, 1, d), lambda i, *_: ((layer * 8 + mod_of_tile(i)) * N_MOD + which, 0, 0))


def norm_mod(x, gain, mods, layer, which_scale, which_shift, mod_of_tile, tm=ROW_TILE):
    t, d = x.shape
    return pl.pallas_call(
        _normmod_kernel,
        grid=(t // tm,),
        in_specs=[
            pl.BlockSpec((tm, d), lambda i: (i, 0)),
            pl.BlockSpec((None, 1, d), lambda i: (layer, 0, 0)),
            _mod_spec(d, layer, which_scale, mod_of_tile),
            _mod_spec(d, layer, which_shift, mod_of_tile),
        ],
        out_specs=pl.BlockSpec((tm, d), lambda i: (i, 0)),
        out_shape=jax.ShapeDtypeStruct((t, d), BF16),
        compiler_params=_params("arbitrary"),
        name="norm_mod",
    )(x, gain, mods, mods)


def _mm_kernel(x_ref, w_ref, o_ref, wbf_ref):
    @pl.when(pl.program_id(1) == 0)
    def _():
        wbf_ref[...] = w_ref[...].astype(BF16)

    o_ref[...] = jnp.dot(x_ref[...], wbf_ref[...], preferred_element_type=F32).astype(o_ref.dtype)


def proj_in(h, w_in, layer, tm=1024, tn=512):
    t, d = h.shape
    n = w_in.shape[-1]
    tm, tn = min(tm, t), min(tn, n)
    return pl.pallas_call(
        _mm_kernel,
        grid=(n // tn, t // tm),
        in_specs=[
            pl.BlockSpec((tm, d), lambda j, i: (i, 0)),
            pl.BlockSpec((None, d, tn), lambda j, i: (layer, 0, j)),
        ],
        out_specs=pl.BlockSpec((tm, tn), lambda j, i: (i, j)),
        out_shape=jax.ShapeDtypeStruct((t, n), F32),
        scratch_shapes=[pltpu.VMEM((d, tn), BF16)],
        compiler_params=_params("arbitrary", "arbitrary"),
        name="proj_in",
    )(h, w_in)


def _proj_out_kernel(a_ref, b_ref, c_ref, d_ref, w_ref, x_ref, g_ref, o_ref, wbf_ref):
    @pl.when(pl.program_id(1) == 0)
    def _():
        wbf_ref[...] = w_ref[...].astype(BF16)

    k = a_ref.shape[1]
    acc = jnp.dot(a_ref[...], wbf_ref[0:k, :], preferred_element_type=F32)
    acc += jnp.dot(b_ref[...], wbf_ref[k:2 * k, :], preferred_element_type=F32)
    acc += jnp.dot(c_ref[...], wbf_ref[2 * k:3 * k, :], preferred_element_type=F32)
    acc += jnp.dot(d_ref[...], wbf_ref[3 * k:4 * k, :], preferred_element_type=F32)
    o_ref[...] = x_ref[...] + g_ref[...] * acc


def proj_out_residual(parts, w_out, x, mods, layer, mod_of_tile, tm=1024, tn=512):
    t, d = x.shape
    k = parts[0].shape[1]
    tm, tn = min(tm, t), min(tn, d)
    part_spec = pl.BlockSpec((tm, k), lambda j, i: (i, 0))
    return pl.pallas_call(
        _proj_out_kernel,
        grid=(d // tn, t // tm),
        in_specs=[part_spec] * 4 + [
            pl.BlockSpec((None, d, tn), lambda j, i: (layer, 0, j)),
            pl.BlockSpec((tm, tn), lambda j, i: (i, j)),
            pl.BlockSpec((None, 1, tn),
                         lambda j, i: ((layer * 8 + mod_of_tile(i)) * N_MOD + 2, 0, j)),
        ],
        out_specs=pl.BlockSpec((tm, tn), lambda j, i: (i, j)),
        out_shape=jax.ShapeDtypeStruct((t, d), F32),
        scratch_shapes=[pltpu.VMEM((d, tn), BF16)],
        compiler_params=_params("arbitrary", "arbitrary"),
        name="proj_out",
    )(*parts, w_out, x, mods)


def _router_kernel(x_ref, g_ref, sc_ref, sh_ref, rw_ref, rb_ref,
                   h_ref, idx_ref, gate_ref, rank_ref, cnt_ref, carry_ref):
    i = pl.program_id(0)

    @pl.when(i == 0)
    def _():
        carry_ref[...] = jnp.zeros_like(carry_ref)

    h = (_rms(x_ref[...]) * g_ref[...]) * (1.0 + sc_ref[...]) + sh_ref[...]
    h_ref[...] = h
    logits = jnp.dot(h, rw_ref[...], preferred_element_type=F32,
                     precision=lax.Precision.HIGHEST) + rb_ref[...]
    tm = logits.shape[0]
    lane = lax.broadcasted_iota(jnp.int32, (tm, LANES), 1)
    vals, hots = [], []
    idx_out = jnp.zeros((tm, LANES), jnp.int32)
    for j in range(TOP_K):
        m = jnp.max(logits, axis=-1, keepdims=True)
        idx = jnp.min(jnp.where(logits == m, lane, LANES), axis=-1, keepdims=True)
        hot = lane == idx
        vals.append(m)
        hots.append(hot)
        idx_out = jnp.where(lane == j, idx, idx_out)
        logits = jnp.where(hot, -jnp.inf, logits)
    exps = [jnp.exp(v - vals[0]) for v in vals]
    denom = exps[0] + exps[1] + exps[2] + exps[3]
    gate_out = jnp.zeros((tm, LANES), F32)
    for j in range(TOP_K):
        gate_out = jnp.where(lane == j, exps[j] / denom, gate_out)
    chosen = (hots[0] | hots[1] | hots[2] | hots[3])
    chosen_f = jnp.where(chosen, 1.0, 0.0)
    row = lax.broadcasted_iota(jnp.int32, (tm, tm), 0)
    col = lax.broadcasted_iota(jnp.int32, (tm, tm), 1)
    before = jnp.where(col < row, 1.0, 0.0).astype(BF16)
    base = carry_ref[...] + jnp.dot(before, chosen_f.astype(BF16), preferred_element_type=F32)
    rank_out = jnp.zeros((tm, LANES), F32)
    for j in range(TOP_K):
        r = jnp.sum(jnp.where(hots[j], base, 0.0), axis=-1, keepdims=True)
        rank_out = jnp.where(lane == j, r, rank_out)
    carry_ref[...] += jnp.sum(chosen_f, axis=0, keepdims=True)
    idx_ref[...] = idx_out
    gate_ref[...] = gate_out
    rank_ref[...] = rank_out.astype(jnp.int32)
    cnt_ref[...] = carry_ref[...].astype(jnp.int32)


def route(x, gain, mods, router_w, router_b, layer, mod_of_tile, tm=ROW_TILE):
    t, d = x.shape
    ne = router_w.shape[-1]
    rw = jnp.pad(router_w[layer], ((0, 0), (0, LANES - ne)))
    rb = jnp.pad(router_b[layer], (0, LANES - ne), constant_values=-jnp.inf).reshape(1, LANES)
    tile = pl.BlockSpec((tm, LANES), lambda i: (i, 0))
    return pl.pallas_call(
        _router_kernel,
        grid=(t // tm,),
        in_specs=[
            pl.BlockSpec((tm, d), lambda i: (i, 0)),
            pl.BlockSpec((None, 1, d), lambda i: (layer, 0, 0)),
            _mod_spec(d, layer, 4, mod_of_tile),
            _mod_spec(d, layer, 3, mod_of_tile),
            pl.BlockSpec((d, LANES), lambda i: (0, 0)),
            pl.BlockSpec((1, LANES), lambda i: (0, 0)),
        ],
        out_specs=[pl.BlockSpec((tm, d), lambda i: (i, 0)), tile, tile, tile,
                   pl.BlockSpec((1, LANES), lambda i: (0, 0))],
        out_shape=[jax.ShapeDtypeStruct((t, d), F32),
                   jax.ShapeDtypeStruct((t, LANES), jnp.int32),
                   jax.ShapeDtypeStruct((t, LANES), F32),
                   jax.ShapeDtypeStruct((t, LANES), jnp.int32),
                   jax.ShapeDtypeStruct((1, LANES), jnp.int32)],
        scratch_shapes=[pltpu.VMEM((1, LANES), F32)],
        compiler_params=_params("arbitrary"),
        name="route",
    )(x, gain, mods, mods, rw, rb)


def _gather_rows_kernel(tok_ref, nact_ref, h_hbm, o_ref, buf_ref, sem):
    i = pl.program_id(0)
    rows = buf_ref.shape[0]

    @pl.when(i < nact_ref[0])
    def _():
        def issue(r, carry):
            tok = tok_ref[i * rows + r]
            pltpu.make_async_copy(h_hbm.at[pl.ds(tok, 1), :], buf_ref.at[pl.ds(r, 1), :], sem).start()
            return carry

        lax.fori_loop(0, rows, issue, 0)
        pltpu.make_async_copy(h_hbm.at[pl.ds(0, rows), :], buf_ref, sem).wait()
        o_ref[...] = buf_ref[...].astype(o_ref.dtype)

    @pl.when(i >= nact_ref[0])
    def _():
        o_ref[...] = jnp.zeros_like(o_ref)


def gather_rows(h, slot_tok, n_active, n_blocks, tm=ROW_TILE):
    t, d = h.shape
    grid_spec = pltpu.PrefetchScalarGridSpec(
        num_scalar_prefetch=2,
        grid=(n_blocks,),
        in_specs=[pl.BlockSpec(memory_space=pl.ANY)],
        out_specs=pl.BlockSpec((tm, d), lambda i, tok, na: (i, 0)),
        scratch_shapes=[pltpu.VMEM((tm, d), F32), pltpu.SemaphoreType.DMA],
    )
    return pl.pallas_call(
        _gather_rows_kernel,
        grid_spec=grid_spec,
        out_shape=jax.ShapeDtypeStruct((n_blocks * tm, d), BF16),
        compiler_params=_params("arbitrary"),
        name="moe_gather",
    )(slot_tok, n_active, h)


def _expert_changed(i, be_ref):
    return (i == 0) | (be_ref[i] != be_ref[jnp.maximum(i - 1, 0)])


def _gmm_up_kernel(be_ref, nact_ref, x_ref, wg_ref, wu_ref, bg_ref, bu_ref, o_ref, wg_bf, wu_bf):
    i = pl.program_id(1)

    @pl.when(i < nact_ref[0])
    def _():
        @pl.when(_expert_changed(i, be_ref))
        def _():
            wg_bf[...] = wg_ref[...].astype(BF16)
            wu_bf[...] = wu_ref[...].astype(BF16)

        x = x_ref[...]
        g = jnp.dot(x, wg_bf[...], preferred_element_type=F32) + bg_ref[...]
        u = jnp.dot(x, wu_bf[...], preferred_element_type=F32) + bu_ref[...]
        gate = jnp.minimum(g, SWIGLU_LIMIT)
        up = jnp.clip(u, -SWIGLU_LIMIT, SWIGLU_LIMIT)
        act = (up + 1.0) * gate * _sigmoid(gate * SWIGLU_ALPHA)
        o_ref[...] = act.astype(o_ref.dtype)

    @pl.when(i >= nact_ref[0])
    def _():
        o_ref[...] = jnp.zeros_like(o_ref)


def gmm_up(xs, w_gu, b_gu, block_expert, n_active, layer, tm=ROW_TILE, tn=1024):
    p, d = xs.shape
    dff = w_gu.shape[-1] // 2
    tn = min(tn, dff)
    nb, nc = p // tm, dff // tn
    depth, ne = b_gu.shape[:2]
    b4 = b_gu.reshape(depth, ne, 1, 2 * dff)

    def blk(i, na):
        return jnp.minimum(i, na[0] - 1)

    grid_spec = pltpu.PrefetchScalarGridSpec(
        num_scalar_prefetch=2,
        grid=(nc, nb),
        in_specs=[
            pl.BlockSpec((tm, d), lambda c, i, be, na: (blk(i, na), 0)),
            pl.BlockSpec((None, None, d, tn), lambda c, i, be, na: (layer, be[blk(i, na)], 0, c)),
            pl.BlockSpec((None, None, d, tn), lambda c, i, be, na: (layer, be[blk(i, na)], 0, nc + c)),
            pl.BlockSpec((None, None, 1, tn), lambda c, i, be, na: (layer, be[blk(i, na)], 0, c)),
            pl.BlockSpec((None, None, 1, tn), lambda c, i, be, na: (layer, be[blk(i, na)], 0, nc + c)),
        ],
        out_specs=pl.BlockSpec((tm, tn), lambda c, i, be, na: (i, c)),
        scratch_shapes=[pltpu.VMEM((d, tn), BF16), pltpu.VMEM((d, tn), BF16)],
    )
    return pl.pallas_call(
        _gmm_up_kernel,
        grid_spec=grid_spec,
        out_shape=jax.ShapeDtypeStruct((p, dff), BF16),
        compiler_params=_params("arbitrary", "arbitrary"),
        name="moe_up",
    )(block_expert, n_active, xs, w_gu, w_gu, b4, b4)


def _gmm_down_kernel(be_ref, nact_ref, a_ref, w_ref, b_ref, o_ref, w_bf):
    i = pl.program_id(1)

    @pl.when(i < nact_ref[0])
    def _():
        @pl.when(_expert_changed(i, be_ref))
        def _():
            w_bf[...] = w_ref[...].astype(BF16)

        o_ref[...] = jnp.dot(a_ref[...], w_bf[...], preferred_element_type=F32) + b_ref[...]

    @pl.when(i >= nact_ref[0])
    def _():
        o_ref[...] = jnp.zeros_like(o_ref)


def gmm_down(act, w_dn, b_dn, block_expert, n_active, layer, tm=ROW_TILE, tn=1024):
    p, dff = act.shape
    d = w_dn.shape[-1]
    tn = min(tn, d)
    nb, nc = p // tm, d // tn
    depth, ne = b_dn.shape[:2]
    b4 = b_dn.reshape(depth, ne, 1, d)

    def blk(i, na):
        return jnp.minimum(i, na[0] - 1)

    grid_spec = pltpu.PrefetchScalarGridSpec(
        num_scalar_prefetch=2,
        grid=(nc, nb),
        in_specs=[
            pl.BlockSpec((tm, dff), lambda c, i, be, na: (blk(i, na), 0)),
            pl.BlockSpec((None, None, dff, tn), lambda c, i, be, na: (layer, be[blk(i, na)], 0, c)),
            pl.BlockSpec((None, None, 1, tn), lambda c, i, be, na: (layer, be[blk(i, na)], 0, c)),
        ],
        out_specs=pl.BlockSpec((tm, tn), lambda c, i, be, na: (i, c)),
        scratch_shapes=[pltpu.VMEM((dff, tn), BF16)],
    )
    return pl.pallas_call(
        _gmm_down_kernel,
        grid_spec=grid_spec,
        out_shape=jax.ShapeDtypeStruct((p, d), F32),
        compiler_params=_params("arbitrary", "arbitrary"),
        name="moe_down",
    )(block_expert, n_active, act, w_dn, b4)


def _combine_kernel(dest_ref, y_hbm, x_ref, gate_ref, g2_ref, o_ref, buf_ref, sem):
    i = pl.program_id(0)
    tm = x_ref.shape[0]

    def issue(r, carry):
        for j in range(TOP_K):
            slot = dest_ref[(i * tm + r) * TOP_K + j]
            pltpu.make_async_copy(y_hbm.at[pl.ds(slot, 1), :],
                                  buf_ref.at[j, pl.ds(r, 1), :], sem).start()
        return carry

    lax.fori_loop(0, tm, issue, 0)
    for j in range(TOP_K):
        pltpu.make_async_copy(y_hbm.at[pl.ds(0, tm), :], buf_ref.at[j], sem).wait()
    gates = gate_ref[...]
    acc = gates[:, 0:1] * buf_ref[0]
    for j in range(1, TOP_K):
        acc += gates[:, j:j + 1] * buf_ref[j]
    o_ref[...] = x_ref[...] + g2_ref[...] * acc


def combine_residual(y, dest, gates, x, mods, layer, mod_of_tile, tm=ROW_TILE):
    t, d = x.shape
    grid_spec = pltpu.PrefetchScalarGridSpec(
        num_scalar_prefetch=1,
        grid=(t // tm,),
        in_specs=[
            pl.BlockSpec(memory_space=pl.ANY),
            pl.BlockSpec((tm, d), lambda i, dst: (i, 0)),
            pl.BlockSpec((tm, LANES), lambda i, dst: (i, 0)),
            _mod_spec(d, layer, 5, mod_of_tile),
        ],
        out_specs=pl.BlockSpec((tm, d), lambda i, dst: (i, 0)),
        scratch_shapes=[pltpu.VMEM((TOP_K, tm, d), F32), pltpu.SemaphoreType.DMA],
    )
    return pl.pallas_call(
        _combine_kernel,
        grid_spec=grid_spec,
        out_shape=jax.ShapeDtypeStruct((t, d), F32),
        compiler_params=_params("arbitrary"),
        name="moe_combine",
    )(dest, y, x, gates, mods)


def moe_layer(x, gain, mods, router_w, router_b, w_gu, b_gu, w_dn, b_dn, layer, mod_of_tile):
    t, d = x.shape
    ne = router_w.shape[-1]
    n_blocks = t * TOP_K // ROW_TILE + ne
    h, idx, gates, rank, counts = route(x, gain, mods, router_w, router_b, layer, mod_of_tile)
    counts = counts[0, :ne]
    padded = (counts + ROW_TILE - 1) // ROW_TILE * ROW_TILE
    ends = jnp.cumsum(padded)
    pstart = ends - padded
    e_idx = idx[:, :TOP_K]
    dest = (pstart[e_idx] + rank[:, :TOP_K]).reshape(t * TOP_K)
    tok = jnp.arange(t * TOP_K, dtype=jnp.int32) // TOP_K
    slot_tok = jnp.zeros((n_blocks * ROW_TILE,), jnp.int32).at[dest].set(tok)
    block_expert = jnp.minimum(
        jnp.searchsorted(ends, jnp.arange(n_blocks, dtype=jnp.int32) * ROW_TILE, side='right'),
        ne - 1).astype(jnp.int32)
    n_active = (ends[-1:] // ROW_TILE).astype(jnp.int32)
    xs = gather_rows(h, slot_tok, n_active, n_blocks)
    act = gmm_up(xs, w_gu, b_gu, block_expert, n_active, layer)
    y = gmm_down(act, w_dn, b_dn, block_expert, n_active, layer)
    return combine_residual(y, dest, gates, x, mods, layer, mod_of_tile)


def _final_norm_kernel(x_ref, g_ref, o_ref):
    o_ref[...] = _rms(x_ref[...]) * g_ref[...]


def final_norm(x, gain, tm=ROW_TILE):
    t, d = x.shape
    return pl.pallas_call(
        _final_norm_kernel,
        grid=(t // tm,),
        in_specs=[pl.BlockSpec((tm, d), lambda i: (i, 0)), pl.BlockSpec((1, d), lambda i: (0, 0))],
        out_specs=pl.BlockSpec((tm, d), lambda i: (i, 0)),
        out_shape=jax.ShapeDtypeStruct((t, d), F32),
        compiler_params=_params("arbitrary"),
        name="final_norm",
    )(x, gain.reshape(1, d))


def _context_attention(q, k, v):
    s = jnp.einsum('bqhd,bkhd->bhqk', q, k).astype(F32) * (HEAD_DIM ** -0.5)
    p = jax.nn.softmax(s, axis=-1)
    return jnp.einsum('bhqk,bkhd->bqhd', p, v)


def _na_bias(rpb, rows):
    wr = min(NA_WIN_ROWS, rows)
    pos = np.arange(rows * GRID_W)
    r, c = pos // GRID_W, pos % GRID_W
    r0 = np.clip(r - wr // 2, 0, rows - wr)
    c0 = np.clip(c - NA_WIN_COLS // 2, 0, GRID_W - NA_WIN_COLS)
    kr, kc = r[None, :], c[None, :]
    inside = ((kr >= r0[:, None]) & (kr < r0[:, None] + wr)
              & (kc >= c0[:, None]) & (kc < c0[:, None] + NA_WIN_COLS))
    d_row = np.clip(kr - r[:, None] + NA_WIN_ROWS - 1, 0, 2 * NA_WIN_ROWS - 2)
    d_col = np.clip(kc - c[:, None] + NA_WIN_COLS - 1, 0, 2 * NA_WIN_COLS - 2)
    bias = rpb[:, d_row, d_col]
    return jnp.where(inside[None], bias, NEG_BIG)


def _neighbourhood_attention(q, k, v, ctx_k, ctx_v, rpb):
    b, l, h, hd = q.shape
    bias = _na_bias(rpb, l // GRID_W)
    scale = HEAD_DIM ** -0.5
    s_loc = jnp.einsum('bqhd,bkhd->bhqk', q, k).astype(F32) * scale + bias[None]
    s_ctx = jnp.einsum('bqhd,bkhd->bhqk', q, ctx_k).astype(F32) * scale
    p = jax.nn.softmax(jnp.concatenate([s_loc, s_ctx], axis=-1), axis=-1)
    return (jnp.einsum('bhqk,bkhd->bqhd', p[..., :l], v)
            + jnp.einsum('bhqk,bkhd->bqhd', p[..., l:], ctx_v))


def _spatial_gating(u, v, w_s, b_s):
    b, l, _ = u.shape
    n = l // HEAD_DIM
    u = jax.nn.gelu(u)
    vf = jax.nn.gelu(v).reshape(b, n, HEAD_DIM, N_HEADS, HEAD_DIM)
    vn = vf * lax.rsqrt(jnp.mean(vf * vf, axis=-1, keepdims=True) + EPS)
    mixed = jnp.einsum('gts,bnsgc->bntgc', w_s, vn) + b_s.T[None, None, :, :, None]
    return u * mixed.reshape(b, l, MIX_W)


def _multiscale_pool(x, w_pool, scale):
    b, l, _ = x.shape
    xg = x.reshape(b, l, len(POOL_WINDOWS), -1)
    cs = jnp.pad(jnp.cumsum(xg, axis=1), ((0, 0), (1, 0), (0, 0), (0, 0)))
    t = np.arange(l)
    pooled = []
    for g, w in enumerate(POOL_WINDOWS):
        lo = np.clip(t - w // 2, 0, l)
        hi = np.clip(t + w // 2, 0, l)
        cnt = jnp.asarray((hi - lo).astype(np.float32))[None, :, None]
        csg = cs[:, :, g]
        pooled.append((csg[:, hi] - csg[:, lo]) / cnt - xg[:, :, g])
    pooled = jnp.stack(pooled, axis=2)
    y = jnp.einsum('blgc,gcd->blgd', pooled, w_pool).reshape(b, l, MIX_W)
    return y * scale


def _log_forget(z, lb):
    return jnp.logaddexp(jnp.log(lb + LB_TINY), jnp.log1p(-lb) + jax.nn.log_sigmoid(z))


def _hgrn2_chunk_scan(q, k, v, logf, s0):
    b, l, h, dk = q.shape
    n, c = l // HGRN_CHUNK, HGRN_CHUNK
    rs = lambda a: a.reshape(b, n, c, h, a.shape[-1])
    q, k, v, logf = rs(q), rs(k), rs(v), rs(logf)
    cum = jnp.cumsum(logf, axis=2)
    tri = np.tril(np.ones((c, c), dtype=bool))[None, None, :, :, None, None]
    diff = jnp.where(tri, cum[:, :, :, None] - cum[:, :, None], 0.0)
    decay = jnp.where(tri, jnp.exp(diff), 0.0)
    att = jnp.einsum('bnthk,bnshk,bntshk->bnhts', q, k, decay)
    o_intra = jnp.einsum('bnhts,bnshv->bnthv', att, v)
    last = cum[:, :, -1]
    upd = jnp.einsum('bnshk,bnshv->bnhkv', k * jnp.exp(last[:, :, None] - cum), v)

    def step(s, inp):
        a, u = inp
        return a[..., None] * s + u, s

    s_final, s_starts = lax.scan(step, s0, (jnp.moveaxis(jnp.exp(last), 1, 0), jnp.moveaxis(upd, 1, 0)))
    s_starts = jnp.moveaxis(s_starts, 0, 1)
    o_inter = jnp.einsum('bnthk,bnhkv->bnthv', q * jnp.exp(cum), s_starts)
    return (o_intra + o_inter).reshape(b, l, h, v.shape[-1]), s_final


def _hgrn2_mixer(q_raw, i_raw, ff_raw, fb_raw, g_raw, lb_f, lb_b, norm_g, s0_f, s0_b):
    b, l, _ = q_raw.shape
    heads = lambda a: a.reshape(b, l, N_HEADS, -1)
    flip = lambda a: jnp.flip(a, axis=1)
    q = heads(jax.nn.silu(q_raw))
    v = heads(i_raw)
    logf_f = heads(_log_forget(ff_raw, lb_f))
    logf_b = heads(_log_forget(fb_raw, lb_b))
    o_f, s_f = _hgrn2_chunk_scan(q, -jnp.expm1(logf_f), v, logf_f, s0_f)
    o_b, s_b = _hgrn2_chunk_scan(flip(q), flip(-jnp.expm1(logf_b)), flip(v), flip(logf_b), s0_b)
    o = o_f + flip(o_b)
    o = o * lax.rsqrt(jnp.mean(o * o, axis=-1, keepdims=True) + EPS)
    o = o.reshape(b, l, MIX_W) * norm_g * jax.nn.silu(g_raw)
    return o, s_f, s_b


def token_mixers(proj, bsz, rpb, sgu_w, sgu_b, lb_f, lb_b, hgrn_g, pool_w, pool_s,
                 ctx_k=None, ctx_v=None, s0_f=None, s0_b=None):
    l = proj.shape[0] // bsz
    p = proj.reshape(bsz, l, N_PROJ, MIX_W)
    aq, ak, av, bu, bv, cq, ci, cff, cfb, cg, dx = [p[:, :, i] for i in range(N_PROJ)]
    heads = lambda a: a.reshape(bsz, l, N_HEADS, HEAD_DIM)
    q, k, v = heads(aq), heads(ak), heads(av)
    if ctx_k is None:
        o_a = _context_attention(q, k, v)
        s0_f = s0_b = jnp.zeros((bsz, N_HEADS, HEAD_DIM, HEAD_DIM), F32)
    else:
        o_a = _neighbourhood_attention(q, k, v, ctx_k, ctx_v, rpb)
    o_b = _spatial_gating(bu, bv, sgu_w, sgu_b)
    o_c, s_f, s_b = _hgrn2_mixer(cq, ci, cff, cfb, cg, lb_f, lb_b, hgrn_g, s0_f, s0_b)
    o_d = _multiscale_pool(dx, pool_w, pool_s)
    flat = lambda a: a.reshape(bsz * l, MIX_W).astype(BF16)
    return [flat(o_a), flat(o_b), flat(o_c), flat(o_d)], k, v, s_f, s_b


def kernel(x_prompt, x_sample, cache_k, cache_v, state_hgrn, c, c_ctx, w_mod, b_mod, norm1_g, norm2_g, w_in, na_rpb, sgu_w, sgu_b, hgrn_lb, hgrn_norm_g, pool_w, pool_scale, w_out, router_w, router_b, exp_w_gu, exp_b_gu, exp_w_dn, exp_b_dn, final_norm_g):
    bp, sp, d = x_prompt.shape
    bs, ss, _ = x_sample.shape
    depth = w_mod.shape[0]
    tp, ts = bp * sp, bs * ss

    cond = jnp.zeros((8, d), F32).at[0].set(c_ctx).at[1:1 + bs].set(c)
    mods = adaln_all(cond, w_mod, b_mod).reshape(depth * 8 * N_MOD, 1, d)

    def mod_of_tile(tile_rows):
        prompt_tiles = tp // tile_rows
        per_seq = ss // tile_rows
        return lambda i: jnp.where(i < prompt_tiles, 0, 1 + (i - prompt_tiles) // per_seq)

    lb_p = jax.nn.softmax(hgrn_lb, axis=1)
    lb_all = jnp.maximum(jnp.cumsum(lb_p, axis=1) - lb_p[:, :1], 0.0)

    x = jnp.concatenate([x_prompt.reshape(tp, d), x_sample.reshape(ts, d)], axis=0)
    g1 = norm1_g.reshape(depth, 1, d)
    g2 = norm2_g.reshape(depth, 1, d)
    new_k, new_v, new_s = [], [], []
    for l in range(depth):
        h = norm_mod(x, g1, mods, l, 1, 0, mod_of_tile(ROW_TILE))
        proj = proj_in(h, w_in, l)
        lw = (na_rpb[l], sgu_w[l], sgu_b[l], lb_all[0, l], lb_all[1, l], hgrn_norm_g[l], pool_w[l],
              pool_scale[l])
        parts_p, k_l, v_l, sf_l, sb_l = token_mixers(proj[:tp], bp, *lw)
        parts_s, _, _, _, _ = token_mixers(proj[tp:], bs, *lw, ctx_k=cache_k[:, l], ctx_v=cache_v[:, l],
                                           s0_f=state_hgrn[:, l, 0], s0_b=state_hgrn[:, l, 1])
        parts = [jnp.concatenate([a, b], axis=0) for a, b in zip(parts_p, parts_s)]
        new_k.append(k_l)
        new_v.append(v_l)
        new_s.append(jnp.stack([sf_l, sb_l], axis=1))
        x = proj_out_residual(parts, w_out, x, mods, l, mod_of_tile(1024))
        x = moe_layer(x, g2, mods, router_w, router_b, exp_w_gu, exp_b_gu, exp_w_dn, exp_b_dn, l,
                      mod_of_tile(ROW_TILE))
    y = final_norm(x, final_norm_g)
    return (y[:tp].reshape(bp, sp, d), y[tp:].reshape(bs, ss, d),
            jnp.stack(new_k, axis=1), jnp.stack(new_v, axis=1), jnp.stack(new_s, axis=1))
```

```python
import functools
import math

import numpy as np
import jax
import jax.numpy as jnp
from jax import lax
from jax.experimental import pallas as pl
from jax.experimental.pallas import tpu as pltpu

F32 = jnp.float32
BF16 = jnp.bfloat16

N_MOD = 6
N_PROJ = 11
MIX_W = 512
HEAD_DIM = 128
N_HEADS = 4
GRID_W = 64
NA_WIN_ROWS = 8
NA_WIN_COLS = 16
POOL_WINDOWS = (2, 4, 8, 16)
N_EXPERTS = 32
TOP_K = 4
SWIGLU_ALPHA = 1.702
SWIGLU_LIMIT = 7.0
EPS = 1e-6
LB_TINY = 1e-30
NEG_BIG = -1e30
HGRN_CHUNK = 16

LANES = 128
ROW_TILE = 256
VMEM_LIMIT = 56 * 1024 * 1024


def _params(*sem, vmem=VMEM_LIMIT):
    return pltpu.CompilerParams(dimension_semantics=sem, vmem_limit_bytes=vmem)


def _sigmoid(x):
    return 1.0 / (1.0 + jnp.exp(-x))


def _adaln_kernel(c_ref, w_ref, b_ref, o_ref):
    c = c_ref[...]
    s = (c * _sigmoid(c)).astype(BF16)
    o_ref[...] = jnp.dot(s, w_ref[...].astype(BF16), preferred_element_type=F32) + b_ref[...]


def adaln_all(cond, w_mod, b_mod, tn=1024):
    depth, d, n = w_mod.shape
    rows = cond.shape[0]
    tn = min(tn, n)
    return pl.pallas_call(
        _adaln_kernel,
        grid=(depth, n // tn),
        in_specs=[
            pl.BlockSpec((rows, d), lambda l, j: (0, 0)),
            pl.BlockSpec((None, d, tn), lambda l, j: (l, 0, j)),
            pl.BlockSpec((None, 1, tn), lambda l, j: (l, 0, j)),
        ],
        out_specs=pl.BlockSpec((None, rows, tn), lambda l, j: (l, 0, j)),
        out_shape=jax.ShapeDtypeStruct((depth, rows, n), F32),
        compiler_params=_params("arbitrary", "arbitrary"),
        name="adaln",
    )(cond, w_mod, b_mod.reshape(depth, 1, n))


def _rms(x):
    return x * lax.rsqrt(jnp.mean(x * x, axis=-1, keepdims=True) + EPS)


def _normmod_kernel(x_ref, g_ref, sc_ref, sh_ref, o_ref):
    h = (_rms(x_ref[...]) * g_ref[...]) * (1.0 + sc_ref[...]) + sh_ref[...]
    o_ref[...] = h.astype(o_ref.dtype)


def _mod_spec(d, layer, which, mod_of_tile):
    return pl.BlockSpec((None, 1, d), lambda i, *_: ((layer * 8 + mod_of_tile(i)) * N_MOD + which, 0, 0))


def norm_mod(x, gain, mods, layer, which_scale, which_shift, mod_of_tile, tm=ROW_TILE):
    t, d = x.shape
    return pl.pallas_call(
        _normmod_kernel,
        grid=(t // tm,),
        in_specs=[
            pl.BlockSpec((tm, d), lambda i: (i, 0)),
            pl.BlockSpec((None, 1, d), lambda i: (layer, 0, 0)),
            _mod_spec(d, layer, which_scale, mod_of_tile),
            _mod_spec(d, layer, which_shift, mod_of_tile),
        ],
        out_specs=pl.BlockSpec((tm, d), lambda i: (i, 0)),
        out_shape=jax.ShapeDtypeStruct((t, d), BF16),
        compiler_params=_params("arbitrary"),
        name="norm_mod",
    )(x, gain, mods, mods)


def _mm_kernel(x_ref, w_ref, o_ref, wbf_ref):
    @pl.when(pl.program_id(1) == 0)
    def _():
        wbf_ref[...] = w_ref[...].astype(BF16)

    o_ref[...] = jnp.dot(x_ref[...], wbf_ref[...], preferred_element_type=F32).astype(o_ref.dtype)


def proj_in(h, w_in, layer, tm=1024, tn=512):
    t, d = h.shape
    n = w_in.shape[-1]
    tm, tn = min(tm, t), min(tn, n)
    return pl.pallas_call(
        _mm_kernel,
        grid=(n // tn, t // tm),
        in_specs=[
            pl.BlockSpec((tm, d), lambda j, i: (i, 0)),
            pl.BlockSpec((None, d, tn), lambda j, i: (layer, 0, j)),
        ],
        out_specs=pl.BlockSpec((tm, tn), lambda j, i: (i, j)),
        out_shape=jax.ShapeDtypeStruct((t, n), F32),
        scratch_shapes=[pltpu.VMEM((d, tn), BF16)],
        compiler_params=_params("arbitrary", "arbitrary"),
        name="proj_in",
    )(h, w_in)


def _proj_out_kernel(a_ref, b_ref, c_ref, d_ref, w_ref, x_ref, g_ref, o_ref, wbf_ref):
    @pl.when(pl.program_id(1) == 0)
    def _():
        wbf_ref[...] = w_ref[...].astype(BF16)

    k = a_ref.shape[1]
    acc = jnp.dot(a_ref[...], wbf_ref[0:k, :], preferred_element_type=F32)
    acc += jnp.dot(b_ref[...], wbf_ref[k:2 * k, :], preferred_element_type=F32)
    acc += jnp.dot(c_ref[...], wbf_ref[2 * k:3 * k, :], preferred_element_type=F32)
    acc += jnp.dot(d_ref[...], wbf_ref[3 * k:4 * k, :], preferred_element_type=F32)
    o_ref[...] = x_ref[...] + g_ref[...] * acc


def proj_out_residual(parts, w_out, x, mods, layer, mod_of_tile, tm=1024, tn=512):
    t, d = x.shape
    k = parts[0].shape[1]
    tm, tn = min(tm, t), min(tn, d)
    part_spec = pl.BlockSpec((tm, k), lambda j, i: (i, 0))
    return pl.pallas_call(
        _proj_out_kernel,
        grid=(d // tn, t // tm),
        in_specs=[part_spec] * 4 + [
            pl.BlockSpec((None, d, tn), lambda j, i: (layer, 0, j)),
            pl.BlockSpec((tm, tn), lambda j, i: (i, j)),
            pl.BlockSpec((None, 1, tn),
                         lambda j, i: ((layer * 8 + mod_of_tile(i)) * N_MOD + 2, 0, j)),
        ],
        out_specs=pl.BlockSpec((tm, tn), lambda j, i: (i, j)),
        out_shape=jax.ShapeDtypeStruct((t, d), F32),
        scratch_shapes=[pltpu.VMEM((d, tn), BF16)],
        compiler_params=_params("arbitrary", "arbitrary"),
        name="proj_out",
    )(*parts, w_out, x, mods)


def _router_kernel(x_ref, g_ref, sc_ref, sh_ref, rw_ref, rb_ref,
                   h_ref, idx_ref, gate_ref, rank_ref, cnt_ref, carry_ref):
    i = pl.program_id(0)

    @pl.when(i == 0)
    def _():
        carry_ref[...] = jnp.zeros_like(carry_ref)

    h = (_rms(x_ref[...]) * g_ref[...]) * (1.0 + sc_ref[...]) + sh_ref[...]
    h_ref[...] = h
    logits = jnp.dot(h, rw_ref[...], preferred_element_type=F32,
                     precision=lax.Precision.HIGHEST) + rb_ref[...]
    tm = logits.shape[0]
    lane = lax.broadcasted_iota(jnp.int32, (tm, LANES), 1)
    vals, hots = [], []
    idx_out = jnp.zeros((tm, LANES), jnp.int32)
    for j in range(TOP_K):
        m = jnp.max(logits, axis=-1, keepdims=True)
        idx = jnp.min(jnp.where(logits == m, lane, LANES), axis=-1, keepdims=True)
        hot = lane == idx
        vals.append(m)
        hots.append(hot)
        idx_out = jnp.where(lane == j, idx, idx_out)
        logits = jnp.where(hot, -jnp.inf, logits)
    exps = [jnp.exp(v - vals[0]) for v in vals]
    denom = exps[0] + exps[1] + exps[2] + exps[3]
    gate_out = jnp.zeros((tm, LANES), F32)
    for j in range(TOP_K):
        gate_out = jnp.where(lane == j, exps[j] / denom, gate_out)
    chosen = (hots[0] | hots[1] | hots[2] | hots[3])
    chosen_f = jnp.where(chosen, 1.0, 0.0)
    row = lax.broadcasted_iota(jnp.int32, (tm, tm), 0)
    col = lax.broadcasted_iota(jnp.int32, (tm, tm), 1)
    before = jnp.where(col < row, 1.0, 0.0).astype(BF16)
    base = carry_ref[...] + jnp.dot(before, chosen_f.astype(BF16), preferred_element_type=F32)
    rank_out = jnp.zeros((tm, LANES), F32)
    for j in range(TOP_K):
        r = jnp.sum(jnp.where(hots[j], base, 0.0), axis=-1, keepdims=True)
        rank_out = jnp.where(lane == j, r, rank_out)
    carry_ref[...] += jnp.sum(chosen_f, axis=0, keepdims=True)
    idx_ref[...] = idx_out
    gate_ref[...] = gate_out
    rank_ref[...] = rank_out.astype(jnp.int32)
    cnt_ref[...] = carry_ref[...].astype(jnp.int32)


def route(x, gain, mods, router_w, router_b, layer, mod_of_tile, tm=ROW_TILE):
    t, d = x.shape
    ne = router_w.shape[-1]
    rw = jnp.pad(router_w[layer], ((0, 0), (0, LANES - ne)))
    rb = jnp.pad(router_b[layer], (0, LANES - ne), constant_values=-jnp.inf).reshape(1, LANES)
    tile = pl.BlockSpec((tm, LANES), lambda i: (i, 0))
    return pl.pallas_call(
        _router_kernel,
        grid=(t // tm,),
        in_specs=[
            pl.BlockSpec((tm, d), lambda i: (i, 0)),
            pl.BlockSpec((None, 1, d), lambda i: (layer, 0, 0)),
            _mod_spec(d, layer, 4, mod_of_tile),
            _mod_spec(d, layer, 3, mod_of_tile),
            pl.BlockSpec((d, LANES), lambda i: (0, 0)),
            pl.BlockSpec((1, LANES), lambda i: (0, 0)),
        ],
        out_specs=[pl.BlockSpec((tm, d), lambda i: (i, 0)), tile, tile, tile,
                   pl.BlockSpec((1, LANES), lambda i: (0, 0))],
        out_shape=[jax.ShapeDtypeStruct((t, d), F32),
                   jax.ShapeDtypeStruct((t, LANES), jnp.int32),
                   jax.ShapeDtypeStruct((t, LANES), F32),
                   jax.ShapeDtypeStruct((t, LANES), jnp.int32),
                   jax.ShapeDtypeStruct((1, LANES), jnp.int32)],
        scratch_shapes=[pltpu.VMEM((1, LANES), F32)],
        compiler_params=_params("arbitrary"),
        name="route",
    )(x, gain, mods, mods, rw, rb)


DMA_ISSUE_UNROLL = 8


def _gather_rows_kernel(tok_ref, nact_ref, h_hbm, o_ref, buf_ref, sem):
    i = pl.program_id(0)
    rows = buf_ref.shape[1]
    n_active = nact_ref[0]

    def issue_block(blk):
        slot = blk % 2

        def issue(r, carry):
            tok = tok_ref[blk * rows + r]
            pltpu.make_async_copy(h_hbm.at[pl.ds(tok, 1), :], buf_ref.at[slot, pl.ds(r, 1), :],
                                  sem.at[slot]).start()
            return carry

        lax.fori_loop(0, rows, issue, 0, unroll=DMA_ISSUE_UNROLL)

    @pl.when((i == 0) & (n_active > 0))
    def _():
        issue_block(i)

    @pl.when(i + 1 < n_active)
    def _():
        issue_block(i + 1)

    @pl.when(i < n_active)
    def _():
        slot = i % 2
        pltpu.make_async_copy(h_hbm.at[pl.ds(0, rows), :], buf_ref.at[slot], sem.at[slot]).wait()
        o_ref[...] = buf_ref[slot].astype(o_ref.dtype)

    @pl.when(i >= n_active)
    def _():
        o_ref[...] = jnp.zeros_like(o_ref)


def gather_rows(h, slot_tok, n_active, n_blocks, tm=ROW_TILE):
    t, d = h.shape
    grid_spec = pltpu.PrefetchScalarGridSpec(
        num_scalar_prefetch=2,
        grid=(n_blocks,),
        in_specs=[pl.BlockSpec(memory_space=pl.ANY)],
        out_specs=pl.BlockSpec((tm, d), lambda i, tok, na: (i, 0)),
        scratch_shapes=[pltpu.VMEM((2, tm, d), F32), pltpu.SemaphoreType.DMA((2,))],
    )
    return pl.pallas_call(
        _gather_rows_kernel,
        grid_spec=grid_spec,
        out_shape=jax.ShapeDtypeStruct((n_blocks * tm, d), BF16),
        compiler_params=_params("arbitrary"),
        name="moe_gather",
    )(slot_tok, n_active, h)


def _expert_changed(i, be_ref):
    return (i == 0) | (be_ref[i] != be_ref[jnp.maximum(i - 1, 0)])


def _gmm_up_kernel(be_ref, nact_ref, x_ref, wg_ref, wu_ref, bg_ref, bu_ref, o_ref, wg_bf, wu_bf):
    i = pl.program_id(1)

    @pl.when(i < nact_ref[0])
    def _():
        @pl.when(_expert_changed(i, be_ref))
        def _():
            wg_bf[...] = wg_ref[...].astype(BF16)
            wu_bf[...] = wu_ref[...].astype(BF16)

        x = x_ref[...]
        g = jnp.dot(x, wg_bf[...], preferred_element_type=F32) + bg_ref[...]
        u = jnp.dot(x, wu_bf[...], preferred_element_type=F32) + bu_ref[...]
        gate = jnp.minimum(g, SWIGLU_LIMIT)
        up = jnp.clip(u, -SWIGLU_LIMIT, SWIGLU_LIMIT)
        act = (up + 1.0) * gate * _sigmoid(gate * SWIGLU_ALPHA)
        o_ref[...] = act.astype(o_ref.dtype)

    @pl.when(i >= nact_ref[0])
    def _():
        o_ref[...] = jnp.zeros_like(o_ref)


def gmm_up(xs, w_gu, b_gu, block_expert, n_active, layer, tm=ROW_TILE, tn=1024):
    p, d = xs.shape
    dff = w_gu.shape[-1] // 2
    tn = min(tn, dff)
    nb, nc = p // tm, dff // tn
    depth, ne = b_gu.shape[:2]
    b4 = b_gu.reshape(depth, ne, 1, 2 * dff)

    def blk(i, na):
        return jnp.minimum(i, na[0] - 1)

    grid_spec = pltpu.PrefetchScalarGridSpec(
        num_scalar_prefetch=2,
        grid=(nc, nb),
        in_specs=[
            pl.BlockSpec((tm, d), lambda c, i, be, na: (blk(i, na), 0)),
            pl.BlockSpec((None, None, d, tn), lambda c, i, be, na: (layer, be[blk(i, na)], 0, c)),
            pl.BlockSpec((None, None, d, tn), lambda c, i, be, na: (layer, be[blk(i, na)], 0, nc + c)),
            pl.BlockSpec((None, None, 1, tn), lambda c, i, be, na: (layer, be[blk(i, na)], 0, c)),
            pl.BlockSpec((None, None, 1, tn), lambda c, i, be, na: (layer, be[blk(i, na)], 0, nc + c)),
        ],
        out_specs=pl.BlockSpec((tm, tn), lambda c, i, be, na: (i, c)),
        scratch_shapes=[pltpu.VMEM((d, tn), BF16), pltpu.VMEM((d, tn), BF16)],
    )
    return pl.pallas_call(
        _gmm_up_kernel,
        grid_spec=grid_spec,
        out_shape=jax.ShapeDtypeStruct((p, dff), BF16),
        compiler_params=_params("arbitrary", "arbitrary"),
        name="moe_up",
    )(block_expert, n_active, xs, w_gu, w_gu, b4, b4)


def _gmm_down_kernel(be_ref, nact_ref, a_ref, w_ref, b_ref, o_ref, w_bf):
    i = pl.program_id(1)

    @pl.when(i < nact_ref[0])
    def _():
        @pl.when(_expert_changed(i, be_ref))
        def _():
            w_bf[...] = w_ref[...].astype(BF16)

        o_ref[...] = jnp.dot(a_ref[...], w_bf[...], preferred_element_type=F32) + b_ref[...]

    @pl.when(i >= nact_ref[0])
    def _():
        o_ref[...] = jnp.zeros_like(o_ref)


def gmm_down(act, w_dn, b_dn, block_expert, n_active, layer, tm=ROW_TILE, tn=1024):
    p, dff = act.shape
    d = w_dn.shape[-1]
    tn = min(tn, d)
    nb, nc = p // tm, d // tn
    depth, ne = b_dn.shape[:2]
    b4 = b_dn.reshape(depth, ne, 1, d)

    def blk(i, na):
        return jnp.minimum(i, na[0] - 1)

    grid_spec = pltpu.PrefetchScalarGridSpec(
        num_scalar_prefetch=2,
        grid=(nc, nb),
        in_specs=[
            pl.BlockSpec((tm, dff), lambda c, i, be, na: (blk(i, na), 0)),
            pl.BlockSpec((None, None, dff, tn), lambda c, i, be, na: (layer, be[blk(i, na)], 0, c)),
            pl.BlockSpec((None, None, 1, tn), lambda c, i, be, na: (layer, be[blk(i, na)], 0, c)),
        ],
        out_specs=pl.BlockSpec((tm, tn), lambda c, i, be, na: (i, c)),
        scratch_shapes=[pltpu.VMEM((dff, tn), BF16)],
    )
    return pl.pallas_call(
        _gmm_down_kernel,
        grid_spec=grid_spec,
        out_shape=jax.ShapeDtypeStruct((p, d), F32),
        compiler_params=_params("arbitrary", "arbitrary"),
        name="moe_down",
    )(block_expert, n_active, act, w_dn, b4)


def _combine_kernel(dest_ref, y_hbm, x_ref, gate_ref, g2_ref, o_ref, buf_ref, sem):
    i = pl.program_id(0)
    tm = x_ref.shape[0]

    def issue_block(blk):
        slot = blk % 2

        def issue(r, carry):
            for j in range(TOP_K):
                src = dest_ref[(blk * tm + r) * TOP_K + j]
                pltpu.make_async_copy(y_hbm.at[pl.ds(src, 1), :],
                                      buf_ref.at[slot, j, pl.ds(r, 1), :], sem.at[slot]).start()
            return carry

        lax.fori_loop(0, tm, issue, 0, unroll=DMA_ISSUE_UNROLL // 2)

    @pl.when(i == 0)
    def _():
        issue_block(i)

    @pl.when(i + 1 < pl.num_programs(0))
    def _():
        issue_block(i + 1)

    slot = i % 2
    for j in range(TOP_K):
        pltpu.make_async_copy(y_hbm.at[pl.ds(0, tm), :], buf_ref.at[slot, j], sem.at[slot]).wait()
    gates = gate_ref[...]
    acc = gates[:, 0:1] * buf_ref[slot, 0]
    for j in range(1, TOP_K):
        acc += gates[:, j:j + 1] * buf_ref[slot, j]
    o_ref[...] = x_ref[...] + g2_ref[...] * acc


def combine_residual(y, dest, gates, x, mods, layer, mod_of_tile, tm=ROW_TILE):
    t, d = x.shape
    grid_spec = pltpu.PrefetchScalarGridSpec(
        num_scalar_prefetch=1,
        grid=(t // tm,),
        in_specs=[
            pl.BlockSpec(memory_space=pl.ANY),
            pl.BlockSpec((tm, d), lambda i, dst: (i, 0)),
            pl.BlockSpec((tm, LANES), lambda i, dst: (i, 0)),
            _mod_spec(d, layer, 5, mod_of_tile),
        ],
        out_specs=pl.BlockSpec((tm, d), lambda i, dst: (i, 0)),
        scratch_shapes=[pltpu.VMEM((2, TOP_K, tm, d), F32), pltpu.SemaphoreType.DMA((2,))],
    )
    return pl.pallas_call(
        _combine_kernel,
        grid_spec=grid_spec,
        out_shape=jax.ShapeDtypeStruct((t, d), F32),
        compiler_params=_params("arbitrary"),
        name="moe_combine",
    )(dest, y, x, gates, mods)


def moe_layer(x, gain, mods, router_w, router_b, w_gu, b_gu, w_dn, b_dn, layer, mod_of_tile):
    t, d = x.shape
    ne = router_w.shape[-1]
    n_blocks = t * TOP_K // ROW_TILE + ne
    h, idx, gates, rank, counts = route(x, gain, mods, router_w, router_b, layer, mod_of_tile)
    counts = counts[0, :ne]
    padded = (counts + ROW_TILE - 1) // ROW_TILE * ROW_TILE
    ends = jnp.cumsum(padded)
    pstart = ends - padded
    e_idx = idx[:, :TOP_K]
    dest = (pstart[e_idx] + rank[:, :TOP_K]).reshape(t * TOP_K)
    tok = jnp.arange(t * TOP_K, dtype=jnp.int32) // TOP_K
    slot_tok = jnp.zeros((n_blocks * ROW_TILE,), jnp.int32).at[dest].set(tok)
    block_expert = jnp.minimum(
        jnp.searchsorted(ends, jnp.arange(n_blocks, dtype=jnp.int32) * ROW_TILE, side='right'),
        ne - 1).astype(jnp.int32)
    n_active = (ends[-1:] // ROW_TILE).astype(jnp.int32)
    xs = gather_rows(h, slot_tok, n_active, n_blocks)
    act = gmm_up(xs, w_gu, b_gu, block_expert, n_active, layer)
    y = gmm_down(act, w_dn, b_dn, block_expert, n_active, layer)
    return combine_residual(y, dest, gates, x, mods, layer, mod_of_tile)


def _final_norm_kernel(x_ref, g_ref, o_ref):
    o_ref[...] = _rms(x_ref[...]) * g_ref[...]


def final_norm(x, gain, tm=ROW_TILE):
    t, d = x.shape
    return pl.pallas_call(
        _final_norm_kernel,
        grid=(t // tm,),
        in_specs=[pl.BlockSpec((tm, d), lambda i: (i, 0)), pl.BlockSpec((1, d), lambda i: (0, 0))],
        out_specs=pl.BlockSpec((tm, d), lambda i: (i, 0)),
        out_shape=jax.ShapeDtypeStruct((t, d), F32),
        compiler_params=_params("arbitrary"),
        name="final_norm",
    )(x, gain.reshape(1, d))


ATTN_SCALE = HEAD_DIM ** -0.5
_NT = (((1,), (1,)), ((), ()))
_TN = (((0,), (0,)), ((), ()))


def _head(h):
    return slice(h * HEAD_DIM, (h + 1) * HEAD_DIM)


def _ctx_attn_kernel(q_ref, k_ref, v_ref, o_ref):
    for h in range(N_HEADS):
        q = q_ref[:, _head(h)].astype(BF16)
        k = k_ref[:, _head(h)].astype(BF16)
        v = v_ref[:, _head(h)].astype(BF16)
        s = lax.dot_general(q, k, _NT, preferred_element_type=F32) * ATTN_SCALE
        e = jnp.exp(s - jnp.max(s, axis=-1, keepdims=True))
        p = e / jnp.sum(e, axis=-1, keepdims=True)
        o_ref[:, _head(h)] = jnp.dot(p.astype(BF16), v, preferred_element_type=F32).astype(o_ref.dtype)


def context_attention(proj, n_seq, seq_len):
    def spec(group):
        return pl.BlockSpec((seq_len, MIX_W), lambda b: (b, group))

    return pl.pallas_call(
        _ctx_attn_kernel,
        grid=(n_seq,),
        in_specs=[spec(0), spec(1), spec(2)],
        out_specs=pl.BlockSpec((seq_len, MIX_W), lambda b: (b, 0)),
        out_shape=jax.ShapeDtypeStruct((n_seq * seq_len, MIX_W), BF16),
        compiler_params=_params("arbitrary"),
        name="ctx_attn",
    )(proj, proj, proj)


def na_bias_table(rpb):
    depth, nh = rpb.shape[:2]
    qc = np.arange(GRID_W)[:, None]
    kc = np.arange(GRID_W)[None, :]
    d_col = np.clip(kc - qc + NA_WIN_COLS - 1, 0, 2 * NA_WIN_COLS - 2)
    c0 = np.clip(qc - NA_WIN_COLS // 2, 0, GRID_W - NA_WIN_COLS)
    inside = (kc >= c0) & (kc < c0 + NA_WIN_COLS)
    onehot = (d_col.reshape(-1)[None, :] == np.arange(2 * NA_WIN_COLS - 1)[:, None]).astype(np.float32)
    tt = jnp.einsum('lhij,jm->lhim', rpb, jnp.asarray(onehot), precision=lax.Precision.HIGHEST)
    tt = jnp.where(inside[None, None, None], tt.reshape(depth, nh, -1, GRID_W, GRID_W), NEG_BIG)
    return jnp.stack([jnp.concatenate([tt[:, :, i0 + kk] for kk in range(NA_WIN_ROWS)], axis=-1)
                      for i0 in range(NA_WIN_ROWS)], axis=2)


def _na_attn_kernel(q_ref, k_ref, v_ref, ck_ref, cv_ref, bias_ref, o_ref, *, rows):
    r = pl.program_id(1)
    r0 = jnp.clip(r - NA_WIN_ROWS // 2, 0, rows - NA_WIN_ROWS)
    start = pl.multiple_of(r0 * GRID_W, GRID_W)
    win = NA_WIN_ROWS * GRID_W
    for h in range(N_HEADS):
        q = q_ref[:, _head(h)].astype(BF16)
        kw = k_ref[pl.ds(start, win), _head(h)].astype(BF16)
        vw = v_ref[pl.ds(start, win), _head(h)].astype(BF16)
        ck = ck_ref[:, _head(h)].astype(BF16)
        cv = cv_ref[:, _head(h)].astype(BF16)
        s_loc = lax.dot_general(q, kw, _NT, preferred_element_type=F32) * ATTN_SCALE + bias_ref[h]
        s_ctx = lax.dot_general(q, ck, _NT, preferred_element_type=F32) * ATTN_SCALE
        m = jnp.maximum(jnp.max(s_loc, axis=-1, keepdims=True), jnp.max(s_ctx, axis=-1, keepdims=True))
        e_loc = jnp.exp(s_loc - m)
        e_ctx = jnp.exp(s_ctx - m)
        den = jnp.sum(e_loc, axis=-1, keepdims=True) + jnp.sum(e_ctx, axis=-1, keepdims=True)
        o = (jnp.dot((e_loc / den).astype(BF16), vw, preferred_element_type=F32)
             + jnp.dot((e_ctx / den).astype(BF16), cv, preferred_element_type=F32))
        o_ref[:, _head(h)] = o.astype(o_ref.dtype)


def neighbourhood_attention(proj, cache_k, cache_v, bias_table, layer, row0, n_seq, seq_len):
    rows = seq_len // GRID_W
    assert rows >= NA_WIN_ROWS and row0 % seq_len == 0
    ctx = cache_k.shape[2]
    ck = cache_k.reshape(cache_k.shape[0], cache_k.shape[1], ctx, MIX_W)
    cv = cache_v.reshape(cache_v.shape[0], cache_v.shape[1], ctx, MIX_W)

    def offset_in_window(r):
        return jnp.clip(r - NA_WIN_ROWS // 2, 0, rows - NA_WIN_ROWS) - r + NA_WIN_ROWS - 1

    def seq_spec(group):
        return pl.BlockSpec((seq_len, MIX_W), lambda b, r: (row0 // seq_len + b, group))

    ctx_spec = pl.BlockSpec((None, None, ctx, MIX_W), lambda b, r: (b, layer, 0, 0))
    return pl.pallas_call(
        functools.partial(_na_attn_kernel, rows=rows),
        grid=(n_seq, rows),
        in_specs=[
            pl.BlockSpec((GRID_W, MIX_W), lambda b, r: (row0 // GRID_W + b * rows + r, 0)),
            seq_spec(1), seq_spec(2), ctx_spec, ctx_spec,
            pl.BlockSpec((None, N_HEADS, None, GRID_W, NA_WIN_ROWS * GRID_W),
                         lambda b, r: (layer, 0, offset_in_window(r), 0, 0)),
        ],
        out_specs=pl.BlockSpec((GRID_W, MIX_W), lambda b, r: (b * rows + r, 0)),
        out_shape=jax.ShapeDtypeStruct((n_seq * seq_len, MIX_W), BF16),
        compiler_params=_params("arbitrary", "arbitrary"),
        name="na_attn",
    )(proj, proj, proj, ck, cv, bias_table)


def _gelu_tanh(x):
    return 0.5 * x * (1.0 + jnp.tanh(math.sqrt(2.0 / math.pi) * (x + 0.044715 * (x * x * x))))


def _sgu_kernel(u_ref, v_ref, w_ref, b_ref, o_ref):
    for g in range(N_HEADS):
        u = _gelu_tanh(u_ref[:, _head(g)])
        v = _gelu_tanh(v_ref[:, _head(g)])
        vn = (v * lax.rsqrt(jnp.mean(v * v, axis=-1, keepdims=True) + EPS)).astype(BF16)
        mixed = jnp.dot(w_ref[g].astype(BF16), vn, preferred_element_type=F32) + b_ref[:, _head(g)]
        o_ref[:, _head(g)] = (u * mixed).astype(o_ref.dtype)


def spatial_gating(proj, sgu_w, sgu_b, layer):
    t = proj.shape[0]
    ch = sgu_w.shape[-1]
    depth = sgu_w.shape[0]
    bias = jnp.repeat(jnp.swapaxes(sgu_b, 1, 2), HEAD_DIM, axis=2)
    return pl.pallas_call(
        _sgu_kernel,
        grid=(t // ch,),
        in_specs=[
            pl.BlockSpec((ch, MIX_W), lambda i: (i, 3)),
            pl.BlockSpec((ch, MIX_W), lambda i: (i, 4)),
            pl.BlockSpec((None, N_HEADS, ch, ch), lambda i: (layer, 0, 0, 0)),
            pl.BlockSpec((None, ch, MIX_W), lambda i: (layer, 0, 0)),
        ],
        out_specs=pl.BlockSpec((ch, MIX_W), lambda i: (i, 0)),
        out_shape=jax.ShapeDtypeStruct((t, MIX_W), BF16),
        compiler_params=_params("arbitrary"),
        name="sgu",
    )(proj, proj, sgu_w, bias)


def _pool_kernel(x_ref, w_ref, s_ref, o_ref):
    n = x_ref.shape[0]
    t = lax.broadcasted_iota(jnp.int32, (n, n), 0)
    s = lax.broadcasted_iota(jnp.int32, (n, n), 1)
    tc = lax.broadcasted_iota(jnp.int32, (n, 1), 0)
    for g, win in enumerate(POOL_WINDOWS):
        half = win // 2
        band = jnp.where(s >= t - half, jnp.where(s < t + half, 1.0, 0.0), 0.0).astype(BF16)
        cnt = (jnp.minimum(tc + half, n) - jnp.maximum(tc - half, 0)).astype(F32)
        x = x_ref[:, _head(g)]
        hi = x.astype(BF16)
        lo = (x - hi.astype(F32)).astype(BF16)
        tot = jnp.dot(band, hi, preferred_element_type=F32) + jnp.dot(band, lo, preferred_element_type=F32)
        pooled = tot / cnt - x
        y = jnp.dot(pooled.astype(BF16), w_ref[g].astype(BF16), preferred_element_type=F32)
        o_ref[:, _head(g)] = (y * s_ref[:, _head(g)]).astype(o_ref.dtype)


def multiscale_pool(proj, pool_w, pool_scale, layer, row0, n_seq, seq_len):
    depth = pool_w.shape[0]
    return pl.pallas_call(
        _pool_kernel,
        grid=(n_seq,),
        in_specs=[
            pl.BlockSpec((seq_len, MIX_W), lambda b: (row0 // seq_len + b, N_PROJ - 1)),
            pl.BlockSpec((None, N_HEADS, HEAD_DIM, HEAD_DIM), lambda b: (layer, 0, 0, 0)),
            pl.BlockSpec((None, 1, MIX_W), lambda b: (layer, 0, 0)),
        ],
        out_specs=pl.BlockSpec((seq_len, MIX_W), lambda b: (b, 0)),
        out_shape=jax.ShapeDtypeStruct((n_seq * seq_len, MIX_W), BF16),
        compiler_params=_params("arbitrary"),
        name="pool",
    )(proj, pool_w, pool_scale.reshape(depth, 1, MIX_W))


HGRN_BLOCK = 128
HGRN_DIAG = 8


def _log_forget_k(z, lb):
    a = jnp.log(lb + LB_TINY)
    b = jnp.log1p(-lb) + (jnp.minimum(z, 0.0) - jnp.log1p(jnp.exp(-jnp.abs(z))))
    return jnp.maximum(a, b) + jnp.log1p(jnp.exp(-jnp.abs(a - b)))


def _rows_of(x, idx, span):
    return jnp.concatenate([jnp.broadcast_to(x[i:i + 1, :], (span, x.shape[1])) for i in idx], axis=0)


def _hgrn_block(q, k, v, logf, s0, rev):
    n = HGRN_BLOCK
    row = lax.broadcasted_iota(jnp.int32, (n, n), 0)
    col = lax.broadcasted_iota(jnp.int32, (n, n), 1)
    cum = logf
    sh = 1
    while sh < n:
        if rev:
            cum = cum + jnp.where(row < n - sh, pltpu.roll(cum, n - sh, 0), 0.0)
        else:
            cum = cum + jnp.where(row >= sh, pltpu.roll(cum, sh, 0), 0.0)
        sh *= 2
    att = jnp.zeros((n, n), F32)
    half = n // 2
    while half >= HGRN_DIAG:
        span = 2 * half
        pos = row % span
        first = (pos >= half) if rev else (pos < half)
        edge = half if rev else half - 1
        ref = _rows_of(cum, [j * span + edge for j in range(n // span)], span)
        qe = jnp.where(first, 0.0, q * jnp.exp(jnp.minimum(cum - ref, 0.0)))
        ke = jnp.where(first, k * jnp.exp(jnp.minimum(ref - cum, 0.0)), 0.0)
        a = lax.dot_general(qe.astype(BF16), ke.astype(BF16), _NT, preferred_element_type=F32)
        att = att + jnp.where(row // span == col // span, a, 0.0)
        half //= 2
    pos = row % HGRN_DIAG
    for d in range(HGRN_DIAG):
        if d == 0:
            a = q * k
        else:
            shift = (n - d) if rev else d
            a = q * pltpu.roll(k, shift, 0) * jnp.exp(jnp.minimum(cum - pltpu.roll(cum, shift, 0), 0.0))
            a = jnp.where((pos < HGRN_DIAG - d) if rev else (pos >= d), a, 0.0)
        w = jnp.sum(a, axis=-1, keepdims=True)
        att = att + jnp.where(col == (row + d if rev else row - d), w, 0.0)
    o = jnp.dot(att.astype(BF16), v.astype(BF16), preferred_element_type=F32)
    o = o + jnp.dot((q * jnp.exp(cum)).astype(BF16), s0.astype(BF16), preferred_element_type=F32)
    last = cum[0:1, :] if rev else cum[n - 1:n, :]
    kd = (k * jnp.exp(last - cum)).astype(BF16)
    upd = lax.dot_general(kd, v.astype(BF16), _TN, preferred_element_type=F32)
    keep = jnp.sum(jnp.where(row == col, jnp.exp(last), 0.0), axis=-1, keepdims=True)
    return o, keep * s0 + upd


def _hgrn_kernel(qf_ref, if_ref, ff_ref, qb_ref, ib_ref, fb_ref, lbf_ref, lbb_ref, s0_ref,
                 of_ref, ob_ref, sfin_ref, state_ref, *, step_info):
    from_zero, c, nblk = step_info(pl.program_id(0))[:3]

    @pl.when((c == 0) & from_zero)
    def _():
        state_ref[...] = jnp.zeros_like(state_ref)

    @pl.when((c == 0) & jnp.logical_not(from_zero))
    def _():
        state_ref[...] = s0_ref[...]

    for rev, (q_ref, i_ref, f_ref, lb_ref, o_ref) in enumerate(
            ((qf_ref, if_ref, ff_ref, lbf_ref, of_ref), (qb_ref, ib_ref, fb_ref, lbb_ref, ob_ref))):
        for h in range(N_HEADS):
            q = q_ref[:, _head(h)]
            q = q * _sigmoid(q)
            logf = _log_forget_k(f_ref[:, _head(h)], lb_ref[:, _head(h)])
            k = 1.0 - jnp.exp(logf)
            o, s1 = _hgrn_block(q, k, i_ref[:, _head(h)], logf, state_ref[rev, h], bool(rev))
            o_ref[:, _head(h)] = o
            state_ref[rev, h] = s1

    @pl.when((c == nblk - 1) & from_zero)
    def _():
        sfin_ref[...] = state_ref[...]


def hgrn2_scan(proj, lb_f, lb_b, state_in, layer, n_zero, len_zero, n_init, len_init):
    blk = HGRN_BLOCK
    nz, ni = len_zero // blk, len_init // blk
    steps_zero = n_zero * nz

    def step_info(s):
        from_zero = s < steps_zero
        s2 = s - steps_zero
        b = jnp.where(from_zero, s // nz, s2 // ni)
        c = jnp.where(from_zero, s % nz, s2 % ni)
        nblk = jnp.where(from_zero, nz, ni)
        base = jnp.where(from_zero, b * nz, steps_zero + b * ni)
        return from_zero, c, nblk, b, base

    def fwd(group):
        def index(s):
            _, c, _, _, base = step_info(s)
            return (base + c, group)
        return pl.BlockSpec((blk, MIX_W), index)

    def bwd(group):
        def index(s):
            _, c, nblk, _, base = step_info(s)
            return (base + nblk - 1 - c, group)
        return pl.BlockSpec((blk, MIX_W), index)

    def init_index(s):
        from_zero, _, _, b, _ = step_info(s)
        return (jnp.where(from_zero, 0, b), layer, 0, 0, 0, 0)

    def final_index(s):
        from_zero, _, _, b, _ = step_info(s)
        return (jnp.where(from_zero, b, n_zero - 1), 0, 0, 0, 0)

    t = proj.shape[0]
    state_block = (None, 2, N_HEADS, HEAD_DIM, HEAD_DIM)
    lb_spec = pl.BlockSpec((1, MIX_W), lambda s: (0, 0))
    return pl.pallas_call(
        functools.partial(_hgrn_kernel, step_info=step_info),
        grid=(steps_zero + n_init * ni,),
        in_specs=[fwd(5), fwd(6), fwd(7), bwd(5), bwd(6), bwd(8), lb_spec, lb_spec,
                  pl.BlockSpec((None, None, 2, N_HEADS, HEAD_DIM, HEAD_DIM), init_index)],
        out_specs=[fwd(0), bwd(0), pl.BlockSpec(state_block, final_index)],
        out_shape=[jax.ShapeDtypeStruct((t, MIX_W), F32), jax.ShapeDtypeStruct((t, MIX_W), F32),
                   jax.ShapeDtypeStruct((n_zero, 2, N_HEADS, HEAD_DIM, HEAD_DIM), F32)],
        scratch_shapes=[pltpu.VMEM((2, N_HEADS, HEAD_DIM, HEAD_DIM), F32)],
        compiler_params=_params("arbitrary"),
        name="hgrn_scan",
    )(proj, proj, proj, proj, proj, proj, lb_f.reshape(1, MIX_W), lb_b.reshape(1, MIX_W), state_in)


def _hgrn_out_kernel(of_ref, ob_ref, g_ref, ng_ref, o_ref):
    for h in range(N_HEADS):
        o = of_ref[:, _head(h)] + ob_ref[:, _head(h)]
        o = o * lax.rsqrt(jnp.mean(o * o, axis=-1, keepdims=True) + EPS)
        g = g_ref[:, _head(h)]
        o_ref[:, _head(h)] = (o * ng_ref[:, _head(h)] * (g * _sigmoid(g))).astype(o_ref.dtype)


def hgrn2_output(o_f, o_b, proj, norm_g, layer, tm=512):
    t = o_f.shape[0]
    depth = norm_g.shape[0]
    tile = pl.BlockSpec((tm, MIX_W), lambda i: (i, 0))
    return pl.pallas_call(
        _hgrn_out_kernel,
        grid=(t // tm,),
        in_specs=[tile, tile, pl.BlockSpec((tm, MIX_W), lambda i: (i, 9)),
                  pl.BlockSpec((None, 1, MIX_W), lambda i: (layer, 0, 0))],
        out_specs=tile,
        out_shape=jax.ShapeDtypeStruct((t, MIX_W), BF16),
        compiler_params=_params("arbitrary"),
        name="hgrn_out",
    )(o_f, o_b, proj, norm_g.reshape(depth, 1, MIX_W))


def kernel(x_prompt, x_sample, cache_k, cache_v, state_hgrn, c, c_ctx, w_mod, b_mod, norm1_g, norm2_g, w_in, na_rpb, sgu_w, sgu_b, hgrn_lb, hgrn_norm_g, pool_w, pool_scale, w_out, router_w, router_b, exp_w_gu, exp_b_gu, exp_w_dn, exp_b_dn, final_norm_g):
    bp, sp, d = x_prompt.shape
    bs, ss, _ = x_sample.shape
    depth = w_mod.shape[0]
    tp, ts = bp * sp, bs * ss

    cond = jnp.zeros((8, d), F32).at[0].set(c_ctx).at[1:1 + bs].set(c)
    mods = adaln_all(cond, w_mod, b_mod).reshape(depth * 8 * N_MOD, 1, d)

    def mod_of_tile(tile_rows):
        prompt_tiles = tp // tile_rows
        per_seq = ss // tile_rows
        return lambda i: jnp.where(i < prompt_tiles, 0, 1 + (i - prompt_tiles) // per_seq)

    lb_p = jax.nn.softmax(hgrn_lb, axis=1)
    lb_all = jnp.maximum(jnp.cumsum(lb_p, axis=1) - lb_p[:, :1], 0.0)

    x = jnp.concatenate([x_prompt.reshape(tp, d), x_sample.reshape(ts, d)], axis=0)
    g1 = norm1_g.reshape(depth, 1, d)
    g2 = norm2_g.reshape(depth, 1, d)
    bias_table = na_bias_table(na_rpb)
    new_k, new_v, new_s = [], [], []
    for l in range(depth):
        h = norm_mod(x, g1, mods, l, 1, 0, mod_of_tile(ROW_TILE))
        proj = proj_in(h, w_in, l)
        o_a = jnp.concatenate([context_attention(proj, bp, sp),
                               neighbourhood_attention(proj, cache_k, cache_v, bias_table, l, tp, bs, ss)])
        o_b = spatial_gating(proj, sgu_w, sgu_b, l)
        o_f, o_r, s_l = hgrn2_scan(proj, lb_all[0, l], lb_all[1, l], state_hgrn, l, bp, sp, bs, ss)
        o_c = hgrn2_output(o_f, o_r, proj, hgrn_norm_g, l)
        o_d = jnp.concatenate([multiscale_pool(proj, pool_w, pool_scale, l, 0, bp, sp),
                               multiscale_pool(proj, pool_w, pool_scale, l, tp, bs, ss)])
        new_k.append(proj[:tp, MIX_W:2 * MIX_W].reshape(bp, sp, N_HEADS, HEAD_DIM))
        new_v.append(proj[:tp, 2 * MIX_W:3 * MIX_W].reshape(bp, sp, N_HEADS, HEAD_DIM))
        new_s.append(s_l)
        x = proj_out_residual([o_a, o_b, o_c, o_d], w_out, x, mods, l, mod_of_tile(1024))
        x = moe_layer(x, g2, mods, router_w, router_b, exp_w_gu, exp_b_gu, exp_w_dn, exp_b_dn, l,
                      mod_of_tile(ROW_TILE))
    y = final_norm(x, final_norm_g)
    return (y[:tp].reshape(bp, sp, d), y[tp:].reshape(bs, ss, d),
            jnp.stack(new_k, axis=1), jnp.stack(new_v, axis=1), jnp.stack(new_s, axis=1))
```

```python
import functools
import math

import numpy as np
import jax
import jax.numpy as jnp
from jax import lax
from jax.experimental import pallas as pl
from jax.experimental.pallas import tpu as pltpu

F32 = jnp.float32
BF16 = jnp.bfloat16

N_MOD = 6
N_PROJ = 11
MIX_W = 512
HEAD_DIM = 128
N_HEADS = 4
GRID_W = 64
NA_WIN_ROWS = 8
NA_WIN_COLS = 16
POOL_WINDOWS = (2, 4, 8, 16)
N_EXPERTS = 32
TOP_K = 4
SWIGLU_ALPHA = 1.702
SWIGLU_LIMIT = 7.0
EPS = 1e-6
LB_TINY = 1e-30
NEG_BIG = -1e30
HGRN_CHUNK = 16

LANES = 128
ROW_TILE = 256
VMEM_LIMIT = 56 * 1024 * 1024


def _params(*sem, vmem=VMEM_LIMIT):
    return pltpu.CompilerParams(dimension_semantics=sem, vmem_limit_bytes=vmem)


def _sigmoid(x):
    return 1.0 / (1.0 + jnp.exp(-x))


def _adaln_kernel(c_ref, w_ref, b_ref, o_ref):
    c = c_ref[...]
    s = (c * _sigmoid(c)).astype(BF16)
    o_ref[...] = jnp.dot(s, w_ref[...].astype(BF16), preferred_element_type=F32) + b_ref[...]


def adaln_all(cond, w_mod, b_mod, tn=1024):
    depth, d, n = w_mod.shape
    rows = cond.shape[0]
    tn = min(tn, n)
    return pl.pallas_call(
        _adaln_kernel,
        grid=(depth, n // tn),
        in_specs=[
            pl.BlockSpec((rows, d), lambda l, j: (0, 0)),
            pl.BlockSpec((None, d, tn), lambda l, j: (l, 0, j)),
            pl.BlockSpec((None, 1, tn), lambda l, j: (l, 0, j)),
        ],
        out_specs=pl.BlockSpec((None, rows, tn), lambda l, j: (l, 0, j)),
        out_shape=jax.ShapeDtypeStruct((depth, rows, n), F32),
        compiler_params=_params("arbitrary", "arbitrary"),
        name="adaln",
    )(cond, w_mod, b_mod.reshape(depth, 1, n))


def _rms(x):
    return x * lax.rsqrt(jnp.mean(x * x, axis=-1, keepdims=True) + EPS)


def _normmod_kernel(x_ref, g_ref, sc_ref, sh_ref, o_ref):
    h = (_rms(x_ref[...]) * g_ref[...]) * (1.0 + sc_ref[...]) + sh_ref[...]
    o_ref[...] = h.astype(o_ref.dtype)


def _mod_spec(d, layer, which, mod_of_tile):
    return pl.BlockSpec((None, 1, d), lambda i, *_: ((layer * 8 + mod_of_tile(i)) * N_MOD + which, 0, 0))


def norm_mod(x, gain, mods, layer, which_scale, which_shift, mod_of_tile, tm=ROW_TILE):
    t, d = x.shape
    return pl.pallas_call(
        _normmod_kernel,
        grid=(t // tm,),
        in_specs=[
            pl.BlockSpec((tm, d), lambda i: (i, 0)),
            pl.BlockSpec((None, 1, d), lambda i: (layer, 0, 0)),
            _mod_spec(d, layer, which_scale, mod_of_tile),
            _mod_spec(d, layer, which_shift, mod_of_tile),
        ],
        out_specs=pl.BlockSpec((tm, d), lambda i: (i, 0)),
        out_shape=jax.ShapeDtypeStruct((t, d), BF16),
        compiler_params=_params("arbitrary"),
        name="norm_mod",
    )(x, gain, mods, mods)


def _mm_kernel(x_ref, w_ref, o_ref, wbf_ref):
    @pl.when(pl.program_id(1) == 0)
    def _():
        wbf_ref[...] = w_ref[...].astype(BF16)

    o_ref[...] = jnp.dot(x_ref[...], wbf_ref[...], preferred_element_type=F32).astype(o_ref.dtype)


def proj_in(h, w_in, layer, tm=1024, tn=512):
    t, d = h.shape
    n = w_in.shape[-1]
    tm, tn = min(tm, t), min(tn, n)
    return pl.pallas_call(
        _mm_kernel,
        grid=(n // tn, t // tm),
        in_specs=[
            pl.BlockSpec((tm, d), lambda j, i: (i, 0)),
            pl.BlockSpec((None, d, tn), lambda j, i: (layer, 0, j)),
        ],
        out_specs=pl.BlockSpec((tm, tn), lambda j, i: (i, j)),
        out_shape=jax.ShapeDtypeStruct((t, n), F32),
        scratch_shapes=[pltpu.VMEM((d, tn), BF16)],
        compiler_params=_params("arbitrary", "arbitrary"),
        name="proj_in",
    )(h, w_in)


def _proj_out_kernel(a_ref, b_ref, c_ref, d_ref, w_ref, x_ref, g_ref, o_ref, wbf_ref):
    @pl.when(pl.program_id(1) == 0)
    def _():
        wbf_ref[...] = w_ref[...].astype(BF16)

    k = a_ref.shape[1]
    acc = jnp.dot(a_ref[...], wbf_ref[0:k, :], preferred_element_type=F32)
    acc += jnp.dot(b_ref[...], wbf_ref[k:2 * k, :], preferred_element_type=F32)
    acc += jnp.dot(c_ref[...], wbf_ref[2 * k:3 * k, :], preferred_element_type=F32)
    acc += jnp.dot(d_ref[...], wbf_ref[3 * k:4 * k, :], preferred_element_type=F32)
    o_ref[...] = x_ref[...] + g_ref[...] * acc


def proj_out_residual(parts, w_out, x, mods, layer, mod_of_tile, tm=1024, tn=512):
    t, d = x.shape
    k = parts[0].shape[1]
    tm, tn = min(tm, t), min(tn, d)
    part_spec = pl.BlockSpec((tm, k), lambda j, i: (i, 0))
    return pl.pallas_call(
        _proj_out_kernel,
        grid=(d // tn, t // tm),
        in_specs=[part_spec] * 4 + [
            pl.BlockSpec((None, d, tn), lambda j, i: (layer, 0, j)),
            pl.BlockSpec((tm, tn), lambda j, i: (i, j)),
            pl.BlockSpec((None, 1, tn),
                         lambda j, i: ((layer * 8 + mod_of_tile(i)) * N_MOD + 2, 0, j)),
        ],
        out_specs=pl.BlockSpec((tm, tn), lambda j, i: (i, j)),
        out_shape=jax.ShapeDtypeStruct((t, d), F32),
        scratch_shapes=[pltpu.VMEM((d, tn), BF16)],
        compiler_params=_params("arbitrary", "arbitrary"),
        name="proj_out",
    )(*parts, w_out, x, mods)


def _router_kernel(x_ref, g_ref, sc_ref, sh_ref, rw_ref, rb_ref,
                   h_ref, idx_ref, gate_ref, rank_ref, cnt_ref, carry_ref):
    i = pl.program_id(0)

    @pl.when(i == 0)
    def _():
        carry_ref[...] = jnp.zeros_like(carry_ref)

    h = (_rms(x_ref[...]) * g_ref[...]) * (1.0 + sc_ref[...]) + sh_ref[...]
    h_ref[...] = h
    logits = jnp.dot(h, rw_ref[...], preferred_element_type=F32,
                     precision=lax.Precision.HIGHEST) + rb_ref[...]
    tm = logits.shape[0]
    lane = lax.broadcasted_iota(jnp.int32, (tm, LANES), 1)
    vals, hots = [], []
    idx_out = jnp.zeros((tm, LANES), jnp.int32)
    for j in range(TOP_K):
        m = jnp.max(logits, axis=-1, keepdims=True)
        idx = jnp.min(jnp.where(logits == m, lane, LANES), axis=-1, keepdims=True)
        hot = lane == idx
        vals.append(m)
        hots.append(hot)
        idx_out = jnp.where(lane == j, idx, idx_out)
        logits = jnp.where(hot, -jnp.inf, logits)
    exps = [jnp.exp(v - vals[0]) for v in vals]
    denom = exps[0] + exps[1] + exps[2] + exps[3]
    gate_out = jnp.zeros((tm, LANES), F32)
    for j in range(TOP_K):
        gate_out = jnp.where(lane == j, exps[j] / denom, gate_out)
    chosen = (hots[0] | hots[1] | hots[2] | hots[3])
    chosen_f = jnp.where(chosen, 1.0, 0.0)
    row = lax.broadcasted_iota(jnp.int32, (tm, tm), 0)
    col = lax.broadcasted_iota(jnp.int32, (tm, tm), 1)
    before = jnp.where(col < row, 1.0, 0.0).astype(BF16)
    base = carry_ref[...] + jnp.dot(before, chosen_f.astype(BF16), preferred_element_type=F32)
    rank_out = jnp.zeros((tm, LANES), F32)
    for j in range(TOP_K):
        r = jnp.sum(jnp.where(hots[j], base, 0.0), axis=-1, keepdims=True)
        rank_out = jnp.where(lane == j, r, rank_out)
    carry_ref[...] += jnp.sum(chosen_f, axis=0, keepdims=True)
    idx_ref[...] = idx_out
    gate_ref[...] = gate_out
    rank_ref[...] = rank_out.astype(jnp.int32)
    cnt_ref[...] = carry_ref[...].astype(jnp.int32)


def route(x, gain, mods, router_w, router_b, layer, mod_of_tile, tm=ROW_TILE):
    t, d = x.shape
    ne = router_w.shape[-1]
    rw = jnp.pad(router_w[layer], ((0, 0), (0, LANES - ne)))
    rb = jnp.pad(router_b[layer], (0, LANES - ne), constant_values=-jnp.inf).reshape(1, LANES)
    tile = pl.BlockSpec((tm, LANES), lambda i: (i, 0))
    return pl.pallas_call(
        _router_kernel,
        grid=(t // tm,),
        in_specs=[
            pl.BlockSpec((tm, d), lambda i: (i, 0)),
            pl.BlockSpec((None, 1, d), lambda i: (layer, 0, 0)),
            _mod_spec(d, layer, 4, mod_of_tile),
            _mod_spec(d, layer, 3, mod_of_tile),
            pl.BlockSpec((d, LANES), lambda i: (0, 0)),
            pl.BlockSpec((1, LANES), lambda i: (0, 0)),
        ],
        out_specs=[pl.BlockSpec((tm, d), lambda i: (i, 0)), tile, tile, tile,
                   pl.BlockSpec((1, LANES), lambda i: (0, 0))],
        out_shape=[jax.ShapeDtypeStruct((t, d), F32),
                   jax.ShapeDtypeStruct((t, LANES), jnp.int32),
                   jax.ShapeDtypeStruct((t, LANES), F32),
                   jax.ShapeDtypeStruct((t, LANES), jnp.int32),
                   jax.ShapeDtypeStruct((1, LANES), jnp.int32)],
        scratch_shapes=[pltpu.VMEM((1, LANES), F32)],
        compiler_params=_params("arbitrary"),
        name="route",
    )(x, gain, mods, mods, rw, rb)


DMA_ISSUE_UNROLL = 8


def _dispatch_kernel(dest_ref, fill_ref, nact_ref, h_ref, xs_hbm, zero_ref, sem, zsem, *, n_blocks):
    i = pl.program_id(0)
    tm = h_ref.shape[0]

    def row_copy(src_ref, r, dst, s):
        return pltpu.make_async_copy(src_ref.at[pl.ds(r, 1), :], xs_hbm.at[pl.ds(dst, 1), :], s)

    def issue(r, carry):
        for j in range(TOP_K):
            row_copy(h_ref, r, dest_ref[(i * tm + r) * TOP_K + j], sem).start()
        return carry

    lax.fori_loop(0, tm, issue, 0, unroll=DMA_ISSUE_UNROLL // 2)

    @pl.when(i == pl.num_programs(0) - 1)
    def _():
        zero_ref[...] = jnp.zeros_like(zero_ref)
        n_experts = fill_ref.shape[0] // 2

        def fill_expert(e, carry):
            first, count = fill_ref[2 * e], fill_ref[2 * e + 1]

            def start(r, c):
                row_copy(zero_ref, 0, first + r, zsem).start()
                return c

            def wait(r, c):
                row_copy(zero_ref, 0, first + r, zsem).wait()
                return c

            lax.fori_loop(0, count, start, 0)
            lax.fori_loop(0, count, wait, 0)
            return carry

        lax.fori_loop(0, n_experts, fill_expert, 0)

        def block_copy(b):
            return pltpu.make_async_copy(zero_ref, xs_hbm.at[pl.ds(b * tm, tm), :], zsem)

        def start_block(b, c):
            block_copy(b).start()
            return c

        def wait_block(b, c):
            block_copy(b).wait()
            return c

        lax.fori_loop(nact_ref[0], n_blocks, start_block, 0)
        lax.fori_loop(nact_ref[0], n_blocks, wait_block, 0)

    for j in range(TOP_K):
        pltpu.make_async_copy(h_ref, xs_hbm.at[pl.ds(0, tm), :], sem).wait()


def dispatch_rows(h, dest, fill, n_active, n_blocks, tm=ROW_TILE):
    t, d = h.shape
    grid_spec = pltpu.PrefetchScalarGridSpec(
        num_scalar_prefetch=3,
        grid=(t // tm,),
        in_specs=[pl.BlockSpec((tm, d), lambda i, *_: (i, 0))],
        out_specs=pl.BlockSpec(memory_space=pl.ANY),
        scratch_shapes=[pltpu.VMEM((tm, d), F32), pltpu.SemaphoreType.DMA, pltpu.SemaphoreType.DMA],
    )
    return pl.pallas_call(
        functools.partial(_dispatch_kernel, n_blocks=n_blocks),
        grid_spec=grid_spec,
        out_shape=jax.ShapeDtypeStruct((n_blocks * tm, d), F32),
        compiler_params=_params("arbitrary"),
        name="moe_dispatch",
    )(dest, fill, n_active, h)


def _expert_changed(i, be_ref):
    return (i == 0) | (be_ref[i] != be_ref[jnp.maximum(i - 1, 0)])


def expert_runs(counts, block_expert):
    ne = counts.shape[0]
    ids = jnp.arange(ne, dtype=jnp.int32)
    present = counts > 0
    run_of_expert = jnp.cumsum(present.astype(jnp.int32)) - 1
    later = jnp.where((ids[None, :] > ids[:, None]) & present[None, :], ids[None, :], ne)
    first_present = jnp.min(jnp.where(present, ids, ne))
    nxt = jnp.min(later, axis=1)
    next_of_expert = jnp.where(nxt == ne, first_present, nxt).astype(jnp.int32)
    n_runs = jnp.sum(present.astype(jnp.int32)).reshape(1)
    return run_of_expert[block_expert], next_of_expert[block_expert], n_runs


def _stream_weights(c, i, be_ref, run_ref, next_ref, nruns_ref, copies, casts):
    n_runs = nruns_ref[0]
    g = c * n_runs + run_ref[i]
    slot = g % 2

    @pl.when(g == 0)
    def _():
        for cp in copies(be_ref[i], c, slot):
            cp.start()

    for cp in copies(be_ref[i], c, slot):
        cp.wait()

    @pl.when(g + 1 < pl.num_programs(0) * n_runs)
    def _():
        wraps = run_ref[i] == n_runs - 1
        for cp in copies(next_ref[i], jnp.where(wraps, c + 1, c), 1 - slot):
            cp.start()

    casts(slot)


def _gmm_up_kernel(be_ref, run_ref, next_ref, nruns_ref, nact_ref, x_ref, w_hbm, bg_ref, bu_ref, o_ref,
                   wg_f32, wu_f32, wg_bf, wu_bf, sem, *, layer, tn, dff):
    c = pl.program_id(0)
    i = pl.program_id(1)

    def copies(e, cc, slot):
        col = pl.multiple_of(cc * tn, tn)
        return [pltpu.make_async_copy(w_hbm.at[layer, e, :, pl.ds(col, tn)], wg_f32.at[slot], sem.at[slot]),
                pltpu.make_async_copy(w_hbm.at[layer, e, :, pl.ds(dff + col, tn)], wu_f32.at[slot],
                                      sem.at[slot])]

    def casts(slot):
        wg_bf[...] = wg_f32[slot].astype(BF16)
        wu_bf[...] = wu_f32[slot].astype(BF16)

    @pl.when(i < nact_ref[0])
    def _():
        @pl.when(_expert_changed(i, be_ref))
        def _():
            _stream_weights(c, i, be_ref, run_ref, next_ref, nruns_ref, copies, casts)

        x = x_ref[...].astype(BF16)
        g = jnp.dot(x, wg_bf[...], preferred_element_type=F32) + bg_ref[...]
        u = jnp.dot(x, wu_bf[...], preferred_element_type=F32) + bu_ref[...]
        gate = jnp.minimum(g, SWIGLU_LIMIT)
        up = jnp.clip(u, -SWIGLU_LIMIT, SWIGLU_LIMIT)
        act = (up + 1.0) * gate * _sigmoid(gate * SWIGLU_ALPHA)
        o_ref[...] = act.astype(o_ref.dtype)

    @pl.when(i >= nact_ref[0])
    def _():
        o_ref[...] = jnp.zeros_like(o_ref)


def _row_block(i, na):
    return jnp.minimum(i, na[0] - 1)


def gmm_up(xs, w_gu, b_gu, block_expert, runs, n_active, layer, tm=ROW_TILE, tn=1024):
    p, d = xs.shape
    dff = w_gu.shape[-1] // 2
    tn = min(tn, dff)
    nb, nc = p // tm, dff // tn
    depth, ne = b_gu.shape[:2]
    b4 = b_gu.reshape(depth, ne, 1, 2 * dff)
    grid_spec = pltpu.PrefetchScalarGridSpec(
        num_scalar_prefetch=5,
        grid=(nc, nb),
        in_specs=[
            pl.BlockSpec((tm, d), lambda c, i, be, rn, nx, nr, na: (_row_block(i, na), 0)),
            pl.BlockSpec(memory_space=pl.ANY),
            pl.BlockSpec((None, None, 1, tn),
                         lambda c, i, be, rn, nx, nr, na: (layer, be[_row_block(i, na)], 0, c)),
            pl.BlockSpec((None, None, 1, tn),
                         lambda c, i, be, rn, nx, nr, na: (layer, be[_row_block(i, na)], 0, nc + c)),
        ],
        out_specs=pl.BlockSpec((tm, tn), lambda c, i, *_: (i, c)),
        scratch_shapes=[pltpu.VMEM((2, d, tn), F32), pltpu.VMEM((2, d, tn), F32),
                        pltpu.VMEM((d, tn), BF16), pltpu.VMEM((d, tn), BF16),
                        pltpu.SemaphoreType.DMA((2,))],
    )
    return pl.pallas_call(
        functools.partial(_gmm_up_kernel, layer=layer, tn=tn, dff=dff),
        grid_spec=grid_spec,
        out_shape=jax.ShapeDtypeStruct((p, dff), BF16),
        compiler_params=_params("arbitrary", "arbitrary"),
        name="moe_up",
    )(block_expert, *runs, n_active, xs, w_gu, b4, b4)


def _gmm_down_kernel(be_ref, run_ref, next_ref, nruns_ref, nact_ref, a_ref, w_hbm, b_ref, o_ref,
                     w_f32, w_bf, sem, *, layer, tn):
    c = pl.program_id(0)
    i = pl.program_id(1)

    def copies(e, cc, slot):
        col = pl.multiple_of(cc * tn, tn)
        return [pltpu.make_async_copy(w_hbm.at[layer, e, :, pl.ds(col, tn)], w_f32.at[slot], sem.at[slot])]

    def casts(slot):
        w_bf[...] = w_f32[slot].astype(BF16)

    @pl.when(i < nact_ref[0])
    def _():
        @pl.when(_expert_changed(i, be_ref))
        def _():
            _stream_weights(c, i, be_ref, run_ref, next_ref, nruns_ref, copies, casts)

        o_ref[...] = jnp.dot(a_ref[...], w_bf[...], preferred_element_type=F32) + b_ref[...]

    @pl.when(i >= nact_ref[0])
    def _():
        o_ref[...] = jnp.zeros_like(o_ref)


def gmm_down(act, w_dn, b_dn, block_expert, runs, n_active, layer, tm=ROW_TILE, tn=1024):
    p, dff = act.shape
    d = w_dn.shape[-1]
    tn = min(tn, d)
    nb, nc = p // tm, d // tn
    depth, ne = b_dn.shape[:2]
    b4 = b_dn.reshape(depth, ne, 1, d)
    grid_spec = pltpu.PrefetchScalarGridSpec(
        num_scalar_prefetch=5,
        grid=(nc, nb),
        in_specs=[
            pl.BlockSpec((tm, dff), lambda c, i, be, rn, nx, nr, na: (_row_block(i, na), 0)),
            pl.BlockSpec(memory_space=pl.ANY),
            pl.BlockSpec((None, None, 1, tn),
                         lambda c, i, be, rn, nx, nr, na: (layer, be[_row_block(i, na)], 0, c)),
        ],
        out_specs=pl.BlockSpec((tm, tn), lambda c, i, *_: (i, c)),
        scratch_shapes=[pltpu.VMEM((2, dff, tn), F32), pltpu.VMEM((dff, tn), BF16),
                        pltpu.SemaphoreType.DMA((2,))],
    )
    return pl.pallas_call(
        functools.partial(_gmm_down_kernel, layer=layer, tn=tn),
        grid_spec=grid_spec,
        out_shape=jax.ShapeDtypeStruct((p, d), F32),
        compiler_params=_params("arbitrary", "arbitrary"),
        name="moe_down",
    )(block_expert, *runs, n_active, act, w_dn, b4)


def _combine_kernel(dest_ref, y_hbm, x_ref, gate_ref, g2_ref, o_ref, buf_ref, sem):
    i = pl.program_id(0)
    tm = x_ref.shape[0]

    def issue_block(blk):
        slot = blk % 2

        def issue(r, carry):
            for j in range(TOP_K):
                src = dest_ref[(blk * tm + r) * TOP_K + j]
                pltpu.make_async_copy(y_hbm.at[pl.ds(src, 1), :],
                                      buf_ref.at[slot, j, pl.ds(r, 1), :], sem.at[slot]).start()
            return carry

        lax.fori_loop(0, tm, issue, 0, unroll=DMA_ISSUE_UNROLL // 2)

    @pl.when(i == 0)
    def _():
        issue_block(i)

    @pl.when(i + 1 < pl.num_programs(0))
    def _():
        issue_block(i + 1)

    slot = i % 2
    for j in range(TOP_K):
        pltpu.make_async_copy(y_hbm.at[pl.ds(0, tm), :], buf_ref.at[slot, j], sem.at[slot]).wait()
    gates = gate_ref[...]
    acc = gates[:, 0:1] * buf_ref[slot, 0]
    for j in range(1, TOP_K):
        acc += gates[:, j:j + 1] * buf_ref[slot, j]
    o_ref[...] = x_ref[...] + g2_ref[...] * acc


def combine_residual(y, dest, gates, x, mods, layer, mod_of_tile, tm=ROW_TILE):
    t, d = x.shape
    grid_spec = pltpu.PrefetchScalarGridSpec(
        num_scalar_prefetch=1,
        grid=(t // tm,),
        in_specs=[
            pl.BlockSpec(memory_space=pl.ANY),
            pl.BlockSpec((tm, d), lambda i, dst: (i, 0)),
            pl.BlockSpec((tm, LANES), lambda i, dst: (i, 0)),
            _mod_spec(d, layer, 5, mod_of_tile),
        ],
        out_specs=pl.BlockSpec((tm, d), lambda i, dst: (i, 0)),
        scratch_shapes=[pltpu.VMEM((2, TOP_K, tm, d), F32), pltpu.SemaphoreType.DMA((2,))],
    )
    return pl.pallas_call(
        _combine_kernel,
        grid_spec=grid_spec,
        out_shape=jax.ShapeDtypeStruct((t, d), F32),
        compiler_params=_params("arbitrary"),
        name="moe_combine",
    )(dest, y, x, gates, mods)


def moe_layer(x, gain, mods, router_w, router_b, w_gu, b_gu, w_dn, b_dn, layer, mod_of_tile):
    t, d = x.shape
    ne = router_w.shape[-1]
    n_blocks = t * TOP_K // ROW_TILE + ne
    h, idx, gates, rank, counts = route(x, gain, mods, router_w, router_b, layer, mod_of_tile)
    counts = counts[0, :ne]
    padded = (counts + ROW_TILE - 1) // ROW_TILE * ROW_TILE
    ends = jnp.cumsum(padded)
    pstart = ends - padded
    e_idx = idx[:, :TOP_K]
    dest = (pstart[e_idx] + rank[:, :TOP_K]).reshape(t * TOP_K)
    block_start = jnp.arange(n_blocks, dtype=jnp.int32) * ROW_TILE
    block_expert = jnp.minimum(jnp.sum((ends[None, :] <= block_start[:, None]).astype(jnp.int32), axis=1),
                               ne - 1)
    n_active = (ends[-1:] // ROW_TILE).astype(jnp.int32)
    fill = jnp.stack([pstart + counts, padded - counts], axis=1).reshape(2 * ne).astype(jnp.int32)
    runs = expert_runs(counts, block_expert)
    xs = dispatch_rows(h, dest, fill, n_active, n_blocks)
    act = gmm_up(xs, w_gu, b_gu, block_expert, runs, n_active, layer)
    y = gmm_down(act, w_dn, b_dn, block_expert, runs, n_active, layer)
    return combine_residual(y, dest, gates, x, mods, layer, mod_of_tile)


def _final_norm_kernel(x_ref, g_ref, o_ref):
    o_ref[...] = _rms(x_ref[...]) * g_ref[...]


def final_norm(x, gain, tm=ROW_TILE):
    t, d = x.shape
    return pl.pallas_call(
        _final_norm_kernel,
        grid=(t // tm,),
        in_specs=[pl.BlockSpec((tm, d), lambda i: (i, 0)), pl.BlockSpec((1, d), lambda i: (0, 0))],
        out_specs=pl.BlockSpec((tm, d), lambda i: (i, 0)),
        out_shape=jax.ShapeDtypeStruct((t, d), F32),
        compiler_params=_params("arbitrary"),
        name="final_norm",
    )(x, gain.reshape(1, d))


ATTN_SCALE = HEAD_DIM ** -0.5
_NT = (((1,), (1,)), ((), ()))
_TN = (((0,), (0,)), ((), ()))


def _head(h):
    return slice(h * HEAD_DIM, (h + 1) * HEAD_DIM)


def _ctx_attn_kernel(q_ref, k_ref, v_ref, o_ref):
    for h in range(N_HEADS):
        q = q_ref[:, _head(h)].astype(BF16)
        k = k_ref[:, _head(h)].astype(BF16)
        v = v_ref[:, _head(h)].astype(BF16)
        s = lax.dot_general(q, k, _NT, preferred_element_type=F32) * ATTN_SCALE
        e = jnp.exp(s - jnp.max(s, axis=-1, keepdims=True))
        p = e / jnp.sum(e, axis=-1, keepdims=True)
        o_ref[:, _head(h)] = jnp.dot(p.astype(BF16), v, preferred_element_type=F32).astype(o_ref.dtype)


def context_attention(proj, n_seq, seq_len):
    def spec(group):
        return pl.BlockSpec((seq_len, MIX_W), lambda b: (b, group))

    return pl.pallas_call(
        _ctx_attn_kernel,
        grid=(n_seq,),
        in_specs=[spec(0), spec(1), spec(2)],
        out_specs=pl.BlockSpec((seq_len, MIX_W), lambda b: (b, 0)),
        out_shape=jax.ShapeDtypeStruct((n_seq * seq_len, MIX_W), BF16),
        compiler_params=_params("arbitrary"),
        name="ctx_attn",
    )(proj, proj, proj)


def na_bias_table(rpb):
    depth, nh = rpb.shape[:2]
    qc = np.arange(GRID_W)[:, None]
    kc = np.arange(GRID_W)[None, :]
    d_col = np.clip(kc - qc + NA_WIN_COLS - 1, 0, 2 * NA_WIN_COLS - 2)
    c0 = np.clip(qc - NA_WIN_COLS // 2, 0, GRID_W - NA_WIN_COLS)
    inside = (kc >= c0) & (kc < c0 + NA_WIN_COLS)
    onehot = (d_col.reshape(-1)[None, :] == np.arange(2 * NA_WIN_COLS - 1)[:, None]).astype(np.float32)
    tt = jnp.einsum('lhij,jm->lhim', rpb, jnp.asarray(onehot), precision=lax.Precision.HIGHEST)
    tt = jnp.where(inside[None, None, None], tt.reshape(depth, nh, -1, GRID_W, GRID_W), NEG_BIG)
    return jnp.stack([jnp.concatenate([tt[:, :, i0 + kk] for kk in range(NA_WIN_ROWS)], axis=-1)
                      for i0 in range(NA_WIN_ROWS)], axis=2)


def _na_attn_kernel(q_ref, k_ref, v_ref, ck_ref, cv_ref, bias_ref, o_ref, *, rows):
    r = pl.program_id(1)
    r0 = jnp.clip(r - NA_WIN_ROWS // 2, 0, rows - NA_WIN_ROWS)
    start = pl.multiple_of(r0 * GRID_W, GRID_W)
    win = NA_WIN_ROWS * GRID_W
    for h in range(N_HEADS):
        q = q_ref[:, _head(h)].astype(BF16)
        kw = k_ref[pl.ds(start, win), _head(h)].astype(BF16)
        vw = v_ref[pl.ds(start, win), _head(h)].astype(BF16)
        ck = ck_ref[:, _head(h)].astype(BF16)
        cv = cv_ref[:, _head(h)].astype(BF16)
        s_loc = lax.dot_general(q, kw, _NT, preferred_element_type=F32) * ATTN_SCALE + bias_ref[h]
        s_ctx = lax.dot_general(q, ck, _NT, preferred_element_type=F32) * ATTN_SCALE
        m = jnp.maximum(jnp.max(s_loc, axis=-1, keepdims=True), jnp.max(s_ctx, axis=-1, keepdims=True))
        e_loc = jnp.exp(s_loc - m)
        e_ctx = jnp.exp(s_ctx - m)
        den = jnp.sum(e_loc, axis=-1, keepdims=True) + jnp.sum(e_ctx, axis=-1, keepdims=True)
        o = (jnp.dot((e_loc / den).astype(BF16), vw, preferred_element_type=F32)
             + jnp.dot((e_ctx / den).astype(BF16), cv, preferred_element_type=F32))
        o_ref[:, _head(h)] = o.astype(o_ref.dtype)


def neighbourhood_attention(proj, cache_k, cache_v, bias_table, layer, row0, n_seq, seq_len):
    rows = seq_len // GRID_W
    assert rows >= NA_WIN_ROWS and row0 % seq_len == 0
    ctx = cache_k.shape[2]
    ck = cache_k.reshape(cache_k.shape[0], cache_k.shape[1], ctx, MIX_W)
    cv = cache_v.reshape(cache_v.shape[0], cache_v.shape[1], ctx, MIX_W)

    def offset_in_window(r):
        return jnp.clip(r - NA_WIN_ROWS // 2, 0, rows - NA_WIN_ROWS) - r + NA_WIN_ROWS - 1

    def seq_spec(group):
        return pl.BlockSpec((seq_len, MIX_W), lambda b, r: (row0 // seq_len + b, group))

    ctx_spec = pl.BlockSpec((None, None, ctx, MIX_W), lambda b, r: (b, layer, 0, 0))
    return pl.pallas_call(
        functools.partial(_na_attn_kernel, rows=rows),
        grid=(n_seq, rows),
        in_specs=[
            pl.BlockSpec((GRID_W, MIX_W), lambda b, r: (row0 // GRID_W + b * rows + r, 0)),
            seq_spec(1), seq_spec(2), ctx_spec, ctx_spec,
            pl.BlockSpec((None, N_HEADS, None, GRID_W, NA_WIN_ROWS * GRID_W),
                         lambda b, r: (layer, 0, offset_in_window(r), 0, 0)),
        ],
        out_specs=pl.BlockSpec((GRID_W, MIX_W), lambda b, r: (b * rows + r, 0)),
        out_shape=jax.ShapeDtypeStruct((n_seq * seq_len, MIX_W), BF16),
        compiler_params=_params("arbitrary", "arbitrary"),
        name="na_attn",
    )(proj, proj, proj, ck, cv, bias_table)


def _gelu_tanh(x):
    return 0.5 * x * (1.0 + jnp.tanh(math.sqrt(2.0 / math.pi) * (x + 0.044715 * (x * x * x))))


def _sgu_kernel(u_ref, v_ref, w_ref, b_ref, o_ref):
    for g in range(N_HEADS):
        u = _gelu_tanh(u_ref[:, _head(g)])
        v = _gelu_tanh(v_ref[:, _head(g)])
        vn = (v * lax.rsqrt(jnp.mean(v * v, axis=-1, keepdims=True) + EPS)).astype(BF16)
        mixed = jnp.dot(w_ref[g].astype(BF16), vn, preferred_element_type=F32) + b_ref[:, _head(g)]
        o_ref[:, _head(g)] = (u * mixed).astype(o_ref.dtype)


def spatial_gating(proj, sgu_w, sgu_b, layer):
    t = proj.shape[0]
    ch = sgu_w.shape[-1]
    depth = sgu_w.shape[0]
    bias = jnp.repeat(jnp.swapaxes(sgu_b, 1, 2), HEAD_DIM, axis=2)
    return pl.pallas_call(
        _sgu_kernel,
        grid=(t // ch,),
        in_specs=[
            pl.BlockSpec((ch, MIX_W), lambda i: (i, 3)),
            pl.BlockSpec((ch, MIX_W), lambda i: (i, 4)),
            pl.BlockSpec((None, N_HEADS, ch, ch), lambda i: (layer, 0, 0, 0)),
            pl.BlockSpec((None, ch, MIX_W), lambda i: (layer, 0, 0)),
        ],
        out_specs=pl.BlockSpec((ch, MIX_W), lambda i: (i, 0)),
        out_shape=jax.ShapeDtypeStruct((t, MIX_W), BF16),
        compiler_params=_params("arbitrary"),
        name="sgu",
    )(proj, proj, sgu_w, bias)


def _pool_kernel(x_ref, w_ref, s_ref, o_ref):
    n = x_ref.shape[0]
    t = lax.broadcasted_iota(jnp.int32, (n, n), 0)
    s = lax.broadcasted_iota(jnp.int32, (n, n), 1)
    tc = lax.broadcasted_iota(jnp.int32, (n, 1), 0)
    for g, win in enumerate(POOL_WINDOWS):
        half = win // 2
        band = jnp.where(s >= t - half, jnp.where(s < t + half, 1.0, 0.0), 0.0).astype(BF16)
        cnt = (jnp.minimum(tc + half, n) - jnp.maximum(tc - half, 0)).astype(F32)
        x = x_ref[:, _head(g)]
        hi = x.astype(BF16)
        lo = (x - hi.astype(F32)).astype(BF16)
        tot = jnp.dot(band, hi, preferred_element_type=F32) + jnp.dot(band, lo, preferred_element_type=F32)
        pooled = tot / cnt - x
        y = jnp.dot(pooled.astype(BF16), w_ref[g].astype(BF16), preferred_element_type=F32)
        o_ref[:, _head(g)] = (y * s_ref[:, _head(g)]).astype(o_ref.dtype)


def multiscale_pool(proj, pool_w, pool_scale, layer, row0, n_seq, seq_len):
    depth = pool_w.shape[0]
    return pl.pallas_call(
        _pool_kernel,
        grid=(n_seq,),
        in_specs=[
            pl.BlockSpec((seq_len, MIX_W), lambda b: (row0 // seq_len + b, N_PROJ - 1)),
            pl.BlockSpec((None, N_HEADS, HEAD_DIM, HEAD_DIM), lambda b: (layer, 0, 0, 0)),
            pl.BlockSpec((None, 1, MIX_W), lambda b: (layer, 0, 0)),
        ],
        out_specs=pl.BlockSpec((seq_len, MIX_W), lambda b: (b, 0)),
        out_shape=jax.ShapeDtypeStruct((n_seq * seq_len, MIX_W), BF16),
        compiler_params=_params("arbitrary"),
        name="pool",
    )(proj, pool_w, pool_scale.reshape(depth, 1, MIX_W))


HGRN_BLOCK = 128
HGRN_DIAG = 8


def _log_forget_k(z, lb):
    a = jnp.log(lb + LB_TINY)
    b = jnp.log1p(-lb) + (jnp.minimum(z, 0.0) - jnp.log1p(jnp.exp(-jnp.abs(z))))
    return jnp.maximum(a, b) + jnp.log1p(jnp.exp(-jnp.abs(a - b)))


def _rows_of(x, idx, span):
    return jnp.concatenate([jnp.broadcast_to(x[i:i + 1, :], (span, x.shape[1])) for i in idx], axis=0)


def _hgrn_block(q, k, v, logf, s0, rev):
    n = HGRN_BLOCK
    row = lax.broadcasted_iota(jnp.int32, (n, n), 0)
    col = lax.broadcasted_iota(jnp.int32, (n, n), 1)
    cum = logf
    sh = 1
    while sh < n:
        if rev:
            cum = cum + jnp.where(row < n - sh, pltpu.roll(cum, n - sh, 0), 0.0)
        else:
            cum = cum + jnp.where(row >= sh, pltpu.roll(cum, sh, 0), 0.0)
        sh *= 2
    att = jnp.zeros((n, n), F32)
    half = n // 2
    while half >= HGRN_DIAG:
        span = 2 * half
        pos = row % span
        first = (pos >= half) if rev else (pos < half)
        edge = half if rev else half - 1
        ref = _rows_of(cum, [j * span + edge for j in range(n // span)], span)
        qe = jnp.where(first, 0.0, q * jnp.exp(jnp.minimum(cum - ref, 0.0)))
        ke = jnp.where(first, k * jnp.exp(jnp.minimum(ref - cum, 0.0)), 0.0)
        a = lax.dot_general(qe.astype(BF16), ke.astype(BF16), _NT, preferred_element_type=F32)
        att = att + jnp.where(row // span == col // span, a, 0.0)
        half //= 2
    pos = row % HGRN_DIAG
    for d in range(HGRN_DIAG):
        if d == 0:
            a = q * k
        else:
            shift = (n - d) if rev else d
            a = q * pltpu.roll(k, shift, 0) * jnp.exp(jnp.minimum(cum - pltpu.roll(cum, shift, 0), 0.0))
            a = jnp.where((pos < HGRN_DIAG - d) if rev else (pos >= d), a, 0.0)
        w = jnp.sum(a, axis=-1, keepdims=True)
        att = att + jnp.where(col == (row + d if rev else row - d), w, 0.0)
    o = jnp.dot(att.astype(BF16), v.astype(BF16), preferred_element_type=F32)
    o = o + jnp.dot((q * jnp.exp(cum)).astype(BF16), s0.astype(BF16), preferred_element_type=F32)
    last = cum[0:1, :] if rev else cum[n - 1:n, :]
    kd = (k * jnp.exp(last - cum)).astype(BF16)
    upd = lax.dot_general(kd, v.astype(BF16), _TN, preferred_element_type=F32)
    keep = jnp.sum(jnp.where(row == col, jnp.exp(last), 0.0), axis=-1, keepdims=True)
    return o, keep * s0 + upd


def _hgrn_kernel(qf_ref, if_ref, ff_ref, qb_ref, ib_ref, fb_ref, lbf_ref, lbb_ref, s0_ref,
                 of_ref, ob_ref, sfin_ref, state_ref, *, step_info):
    from_zero, c, nblk = step_info(pl.program_id(0))[:3]

    @pl.when((c == 0) & from_zero)
    def _():
        state_ref[...] = jnp.zeros_like(state_ref)

    @pl.when((c == 0) & jnp.logical_not(from_zero))
    def _():
        state_ref[...] = s0_ref[...]

    for rev, (q_ref, i_ref, f_ref, lb_ref, o_ref) in enumerate(
            ((qf_ref, if_ref, ff_ref, lbf_ref, of_ref), (qb_ref, ib_ref, fb_ref, lbb_ref, ob_ref))):
        for h in range(N_HEADS):
            q = q_ref[:, _head(h)]
            q = q * _sigmoid(q)
            logf = _log_forget_k(f_ref[:, _head(h)], lb_ref[:, _head(h)])
            k = 1.0 - jnp.exp(logf)
            o, s1 = _hgrn_block(q, k, i_ref[:, _head(h)], logf, state_ref[rev, h], bool(rev))
            o_ref[:, _head(h)] = o
            state_ref[rev, h] = s1

    @pl.when((c == nblk - 1) & from_zero)
    def _():
        sfin_ref[...] = state_ref[...]


def hgrn2_scan(proj, lb_f, lb_b, state_in, layer, n_zero, len_zero, n_init, len_init):
    blk = HGRN_BLOCK
    nz, ni = len_zero // blk, len_init // blk
    steps_zero = n_zero * nz

    def step_info(s):
        from_zero = s < steps_zero
        s2 = s - steps_zero
        b = jnp.where(from_zero, s // nz, s2 // ni)
        c = jnp.where(from_zero, s % nz, s2 % ni)
        nblk = jnp.where(from_zero, nz, ni)
        base = jnp.where(from_zero, b * nz, steps_zero + b * ni)
        return from_zero, c, nblk, b, base

    def fwd(group):
        def index(s):
            _, c, _, _, base = step_info(s)
            return (base + c, group)
        return pl.BlockSpec((blk, MIX_W), index)

    def bwd(group):
        def index(s):
            _, c, nblk, _, base = step_info(s)
            return (base + nblk - 1 - c, group)
        return pl.BlockSpec((blk, MIX_W), index)

    def init_index(s):
        from_zero, _, _, b, _ = step_info(s)
        return (jnp.where(from_zero, 0, b), layer, 0, 0, 0, 0)

    def final_index(s):
        from_zero, _, _, b, _ = step_info(s)
        return (jnp.where(from_zero, b, n_zero - 1), 0, 0, 0, 0)

    t = proj.shape[0]
    state_block = (None, 2, N_HEADS, HEAD_DIM, HEAD_DIM)
    lb_spec = pl.BlockSpec((1, MIX_W), lambda s: (0, 0))
    return pl.pallas_call(
        functools.partial(_hgrn_kernel, step_info=step_info),
        grid=(steps_zero + n_init * ni,),
        in_specs=[fwd(5), fwd(6), fwd(7), bwd(5), bwd(6), bwd(8), lb_spec, lb_spec,
                  pl.BlockSpec((None, None, 2, N_HEADS, HEAD_DIM, HEAD_DIM), init_index)],
        out_specs=[fwd(0), bwd(0), pl.BlockSpec(state_block, final_index)],
        out_shape=[jax.ShapeDtypeStruct((t, MIX_W), F32), jax.ShapeDtypeStruct((t, MIX_W), F32),
                   jax.ShapeDtypeStruct((n_zero, 2, N_HEADS, HEAD_DIM, HEAD_DIM), F32)],
        scratch_shapes=[pltpu.VMEM((2, N_HEADS, HEAD_DIM, HEAD_DIM), F32)],
        compiler_params=_params("arbitrary"),
        name="hgrn_scan",
    )(proj, proj, proj, proj, proj, proj, lb_f.reshape(1, MIX_W), lb_b.reshape(1, MIX_W), state_in)


def _hgrn_out_kernel(of_ref, ob_ref, g_ref, ng_ref, o_ref):
    for h in range(N_HEADS):
        o = of_ref[:, _head(h)] + ob_ref[:, _head(h)]
        o = o * lax.rsqrt(jnp.mean(o * o, axis=-1, keepdims=True) + EPS)
        g = g_ref[:, _head(h)]
        o_ref[:, _head(h)] = (o * ng_ref[:, _head(h)] * (g * _sigmoid(g))).astype(o_ref.dtype)


def hgrn2_output(o_f, o_b, proj, norm_g, layer, tm=512):
    t = o_f.shape[0]
    depth = norm_g.shape[0]
    tile = pl.BlockSpec((tm, MIX_W), lambda i: (i, 0))
    return pl.pallas_call(
        _hgrn_out_kernel,
        grid=(t // tm,),
        in_specs=[tile, tile, pl.BlockSpec((tm, MIX_W), lambda i: (i, 9)),
                  pl.BlockSpec((None, 1, MIX_W), lambda i: (layer, 0, 0))],
        out_specs=tile,
        out_shape=jax.ShapeDtypeStruct((t, MIX_W), BF16),
        compiler_params=_params("arbitrary"),
        name="hgrn_out",
    )(o_f, o_b, proj, norm_g.reshape(depth, 1, MIX_W))


def kernel(x_prompt, x_sample, cache_k, cache_v, state_hgrn, c, c_ctx, w_mod, b_mod, norm1_g, norm2_g, w_in, na_rpb, sgu_w, sgu_b, hgrn_lb, hgrn_norm_g, pool_w, pool_scale, w_out, router_w, router_b, exp_w_gu, exp_b_gu, exp_w_dn, exp_b_dn, final_norm_g):
    bp, sp, d = x_prompt.shape
    bs, ss, _ = x_sample.shape
    depth = w_mod.shape[0]
    tp, ts = bp * sp, bs * ss

    cond = jnp.zeros((8, d), F32).at[0].set(c_ctx).at[1:1 + bs].set(c)
    mods = adaln_all(cond, w_mod, b_mod).reshape(depth * 8 * N_MOD, 1, d)

    def mod_of_tile(tile_rows):
        prompt_tiles = tp // tile_rows
        per_seq = ss // tile_rows
        return lambda i: jnp.where(i < prompt_tiles, 0, 1 + (i - prompt_tiles) // per_seq)

    lb_p = jax.nn.softmax(hgrn_lb, axis=1)
    lb_all = jnp.maximum(jnp.cumsum(lb_p, axis=1) - lb_p[:, :1], 0.0)

    x = jnp.concatenate([x_prompt.reshape(tp, d), x_sample.reshape(ts, d)], axis=0)
    g1 = norm1_g.reshape(depth, 1, d)
    g2 = norm2_g.reshape(depth, 1, d)
    bias_table = na_bias_table(na_rpb)
    new_k, new_v, new_s = [], [], []
    for l in range(depth):
        h = norm_mod(x, g1, mods, l, 1, 0, mod_of_tile(ROW_TILE))
        proj = proj_in(h, w_in, l)
        o_a = jnp.concatenate([context_attention(proj, bp, sp),
                               neighbourhood_attention(proj, cache_k, cache_v, bias_table, l, tp, bs, ss)])
        o_b = spatial_gating(proj, sgu_w, sgu_b, l)
        o_f, o_r, s_l = hgrn2_scan(proj, lb_all[0, l], lb_all[1, l], state_hgrn, l, bp, sp, bs, ss)
        o_c = hgrn2_output(o_f, o_r, proj, hgrn_norm_g, l)
        o_d = jnp.concatenate([multiscale_pool(proj, pool_w, pool_scale, l, 0, bp, sp),
                               multiscale_pool(proj, pool_w, pool_scale, l, tp, bs, ss)])
        new_k.append(proj[:tp, MIX_W:2 * MIX_W].reshape(bp, sp, N_HEADS, HEAD_DIM))
        new_v.append(proj[:tp, 2 * MIX_W:3 * MIX_W].reshape(bp, sp, N_HEADS, HEAD_DIM))
        new_s.append(s_l)
        x = proj_out_residual([o_a, o_b, o_c, o_d], w_out, x, mods, l, mod_of_tile(1024))
        x = moe_layer(x, g2, mods, router_w, router_b, exp_w_gu, exp_b_gu, exp_w_dn, exp_b_dn, l,
                      mod_of_tile(ROW_TILE))
    y = final_norm(x, final_norm_g)
    return (y[:tp].reshape(bp, sp, d), y[tp:].reshape(bs, ss, d),
            jnp.stack(new_k, axis=1), jnp.stack(new_v, axis=1), jnp.stack(new_s, axis=1))
```

```python
import functools
import math

import numpy as np
import jax
import jax.numpy as jnp
from jax import lax
from jax.experimental import pallas as pl
from jax.experimental.pallas import tpu as pltpu

F32 = jnp.float32
BF16 = jnp.bfloat16

N_MOD = 6
N_PROJ = 11
MIX_W = 512
HEAD_DIM = 128
N_HEADS = 4
GRID_W = 64
NA_WIN_ROWS = 8
NA_WIN_COLS = 16
POOL_WINDOWS = (2, 4, 8, 16)
N_EXPERTS = 32
TOP_K = 4
SWIGLU_ALPHA = 1.702
SWIGLU_LIMIT = 7.0
EPS = 1e-6
LB_TINY = 1e-30
NEG_BIG = -1e30
HGRN_CHUNK = 16

LANES = 128
ROW_TILE = 256
VMEM_LIMIT = 56 * 1024 * 1024


def _params(*sem, vmem=VMEM_LIMIT):
    return pltpu.CompilerParams(dimension_semantics=sem, vmem_limit_bytes=vmem)


def _sigmoid(x):
    return 1.0 / (1.0 + jnp.exp(-x))


def _adaln_kernel(c_ref, w_ref, b_ref, o_ref):
    c = c_ref[...]
    s = (c * _sigmoid(c)).astype(BF16)
    o_ref[...] = jnp.dot(s, w_ref[...].astype(BF16), preferred_element_type=F32) + b_ref[...]


def adaln_all(cond, w_mod, b_mod, tn=1024):
    depth, d, n = w_mod.shape
    rows = cond.shape[0]
    tn = min(tn, n)
    return pl.pallas_call(
        _adaln_kernel,
        grid=(depth, n // tn),
        in_specs=[
            pl.BlockSpec((rows, d), lambda l, j: (0, 0)),
            pl.BlockSpec((None, d, tn), lambda l, j: (l, 0, j)),
            pl.BlockSpec((None, 1, tn), lambda l, j: (l, 0, j)),
        ],
        out_specs=pl.BlockSpec((None, rows, tn), lambda l, j: (l, 0, j)),
        out_shape=jax.ShapeDtypeStruct((depth, rows, n), F32),
        compiler_params=_params("arbitrary", "arbitrary"),
        name="adaln",
    )(cond, w_mod, b_mod.reshape(depth, 1, n))


def _rms(x):
    return x * lax.rsqrt(jnp.mean(x * x, axis=-1, keepdims=True) + EPS)


def _normmod_kernel(x_ref, g_ref, sc_ref, sh_ref, o_ref):
    h = (_rms(x_ref[...]) * g_ref[...]) * (1.0 + sc_ref[...]) + sh_ref[...]
    o_ref[...] = h.astype(o_ref.dtype)


def _mod_spec(d, layer, which, mod_of_tile):
    return pl.BlockSpec((None, 1, d), lambda i, *_: ((layer * 8 + mod_of_tile(i)) * N_MOD + which, 0, 0))


def norm_mod(x, gain, mods, layer, which_scale, which_shift, mod_of_tile, tm=ROW_TILE):
    t, d = x.shape
    return pl.pallas_call(
        _normmod_kernel,
        grid=(t // tm,),
        in_specs=[
            pl.BlockSpec((tm, d), lambda i: (i, 0)),
            pl.BlockSpec((None, 1, d), lambda i: (layer, 0, 0)),
            _mod_spec(d, layer, which_scale, mod_of_tile),
            _mod_spec(d, layer, which_shift, mod_of_tile),
        ],
        out_specs=pl.BlockSpec((tm, d), lambda i: (i, 0)),
        out_shape=jax.ShapeDtypeStruct((t, d), BF16),
        compiler_params=_params("arbitrary"),
        name="norm_mod",
    )(x, gain, mods, mods)


def _mm_kernel(x_ref, w_ref, o_ref, wbf_ref):
    @pl.when(pl.program_id(1) == 0)
    def _():
        wbf_ref[...] = w_ref[...].astype(BF16)

    o_ref[...] = jnp.dot(x_ref[...], wbf_ref[...], preferred_element_type=F32).astype(o_ref.dtype)


def proj_in(h, w_in, layer, tm=1024, tn=512):
    t, d = h.shape
    n = w_in.shape[-1]
    tm, tn = min(tm, t), min(tn, n)
    return pl.pallas_call(
        _mm_kernel,
        grid=(n // tn, t // tm),
        in_specs=[
            pl.BlockSpec((tm, d), lambda j, i: (i, 0)),
            pl.BlockSpec((None, d, tn), lambda j, i: (layer, 0, j)),
        ],
        out_specs=pl.BlockSpec((tm, tn), lambda j, i: (i, j)),
        out_shape=jax.ShapeDtypeStruct((t, n), F32),
        scratch_shapes=[pltpu.VMEM((d, tn), BF16)],
        compiler_params=_params("arbitrary", "arbitrary"),
        name="proj_in",
    )(h, w_in)


def _proj_out_kernel(a_ref, b_ref, c_ref, d_ref, w_ref, x_ref, g_ref, o_ref, wbf_ref):
    @pl.when(pl.program_id(1) == 0)
    def _():
        wbf_ref[...] = w_ref[...].astype(BF16)

    k = a_ref.shape[1]
    acc = jnp.dot(a_ref[...], wbf_ref[0:k, :], preferred_element_type=F32)
    acc += jnp.dot(b_ref[...], wbf_ref[k:2 * k, :], preferred_element_type=F32)
    acc += jnp.dot(c_ref[...], wbf_ref[2 * k:3 * k, :], preferred_element_type=F32)
    acc += jnp.dot(d_ref[...], wbf_ref[3 * k:4 * k, :], preferred_element_type=F32)
    o_ref[...] = x_ref[...] + g_ref[...] * acc


def proj_out_residual(parts, w_out, x, mods, layer, mod_of_tile, tm=1024, tn=512):
    t, d = x.shape
    k = parts[0].shape[1]
    tm, tn = min(tm, t), min(tn, d)
    part_spec = pl.BlockSpec((tm, k), lambda j, i: (i, 0))
    return pl.pallas_call(
        _proj_out_kernel,
        grid=(d // tn, t // tm),
        in_specs=[part_spec] * 4 + [
            pl.BlockSpec((None, d, tn), lambda j, i: (layer, 0, j)),
            pl.BlockSpec((tm, tn), lambda j, i: (i, j)),
            pl.BlockSpec((None, 1, tn),
                         lambda j, i: ((layer * 8 + mod_of_tile(i)) * N_MOD + 2, 0, j)),
        ],
        out_specs=pl.BlockSpec((tm, tn), lambda j, i: (i, j)),
        out_shape=jax.ShapeDtypeStruct((t, d), F32),
        scratch_shapes=[pltpu.VMEM((d, tn), BF16)],
        compiler_params=_params("arbitrary", "arbitrary"),
        name="proj_out",
    )(*parts, w_out, x, mods)


def _pack_bf16_halves(x):
    n = x.shape[1] // 2
    lo = lax.bitcast_convert_type(x[:, :n].astype(BF16).astype(F32), jnp.uint32)
    hi = lax.bitcast_convert_type(x[:, n:].astype(BF16).astype(F32), jnp.uint32)
    return (lo >> 16) | (hi & jnp.uint32(0xFFFF0000))


def _unpack_bf16_halves(w):
    lo = lax.bitcast_convert_type(w << 16, F32).astype(BF16)
    hi = lax.bitcast_convert_type(w & jnp.uint32(0xFFFF0000), F32).astype(BF16)
    return jnp.concatenate([lo, hi], axis=1)


def _router_kernel(x_ref, g_ref, sc_ref, sh_ref, rw_ref, rb_ref,
                   h_ref, idx_ref, gate_ref, rank_ref, cnt_ref, carry_ref):
    i = pl.program_id(0)

    @pl.when(i == 0)
    def _():
        carry_ref[...] = jnp.zeros_like(carry_ref)

    h = (_rms(x_ref[...]) * g_ref[...]) * (1.0 + sc_ref[...]) + sh_ref[...]
    h_ref[...] = _pack_bf16_halves(h)
    logits = jnp.dot(h, rw_ref[...], preferred_element_type=F32,
                     precision=lax.Precision.HIGHEST) + rb_ref[...]
    tm = logits.shape[0]
    lane = lax.broadcasted_iota(jnp.int32, (tm, LANES), 1)
    vals, hots = [], []
    idx_out = jnp.zeros((tm, LANES), jnp.int32)
    for j in range(TOP_K):
        m = jnp.max(logits, axis=-1, keepdims=True)
        idx = jnp.min(jnp.where(logits == m, lane, LANES), axis=-1, keepdims=True)
        hot = lane == idx
        vals.append(m)
        hots.append(hot)
        idx_out = jnp.where(lane == j, idx, idx_out)
        logits = jnp.where(hot, -jnp.inf, logits)
    exps = [jnp.exp(v - vals[0]) for v in vals]
    denom = exps[0] + exps[1] + exps[2] + exps[3]
    gate_out = jnp.zeros((tm, LANES), F32)
    for j in range(TOP_K):
        gate_out = jnp.where(lane == j, exps[j] / denom, gate_out)
    chosen = (hots[0] | hots[1] | hots[2] | hots[3])
    chosen_f = jnp.where(chosen, 1.0, 0.0)
    row = lax.broadcasted_iota(jnp.int32, (tm, tm), 0)
    col = lax.broadcasted_iota(jnp.int32, (tm, tm), 1)
    before = jnp.where(col < row, 1.0, 0.0).astype(BF16)
    base = carry_ref[...] + jnp.dot(before, chosen_f.astype(BF16), preferred_element_type=F32)
    rank_out = jnp.zeros((tm, LANES), F32)
    for j in range(TOP_K):
        r = jnp.sum(jnp.where(hots[j], base, 0.0), axis=-1, keepdims=True)
        rank_out = jnp.where(lane == j, r, rank_out)
    carry_ref[...] += jnp.sum(chosen_f, axis=0, keepdims=True)
    idx_ref[...] = idx_out
    gate_ref[...] = gate_out
    rank_ref[...] = rank_out.astype(jnp.int32)
    cnt_ref[...] = carry_ref[...].astype(jnp.int32)


def route(x, gain, mods, router_w, router_b, layer, mod_of_tile, tm=ROW_TILE):
    t, d = x.shape
    ne = router_w.shape[-1]
    rw = jnp.pad(router_w[layer], ((0, 0), (0, LANES - ne)))
    rb = jnp.pad(router_b[layer], (0, LANES - ne), constant_values=-jnp.inf).reshape(1, LANES)
    tile = pl.BlockSpec((tm, LANES), lambda i: (i, 0))
    return pl.pallas_call(
        _router_kernel,
        grid=(t // tm,),
        in_specs=[
            pl.BlockSpec((tm, d), lambda i: (i, 0)),
            pl.BlockSpec((None, 1, d), lambda i: (layer, 0, 0)),
            _mod_spec(d, layer, 4, mod_of_tile),
            _mod_spec(d, layer, 3, mod_of_tile),
            pl.BlockSpec((d, LANES), lambda i: (0, 0)),
            pl.BlockSpec((1, LANES), lambda i: (0, 0)),
        ],
        out_specs=[pl.BlockSpec((tm, d // 2), lambda i: (i, 0)), tile, tile, tile,
                   pl.BlockSpec((1, LANES), lambda i: (0, 0))],
        out_shape=[jax.ShapeDtypeStruct((t, d // 2), jnp.uint32),
                   jax.ShapeDtypeStruct((t, LANES), jnp.int32),
                   jax.ShapeDtypeStruct((t, LANES), F32),
                   jax.ShapeDtypeStruct((t, LANES), jnp.int32),
                   jax.ShapeDtypeStruct((1, LANES), jnp.int32)],
        scratch_shapes=[pltpu.VMEM((1, LANES), F32)],
        compiler_params=_params("arbitrary"),
        name="route",
    )(x, gain, mods, mods, rw, rb)


DMA_ISSUE_UNROLL = 8


def _dispatch_kernel(dest_ref, fill_ref, nact_ref, h_ref, xs_hbm, zero_ref, sem, zsem, *, n_blocks):
    i = pl.program_id(0)
    tm = h_ref.shape[0]

    def row_copy(src_ref, r, dst, s):
        return pltpu.make_async_copy(src_ref.at[pl.ds(r, 1), :], xs_hbm.at[pl.ds(dst, 1), :], s)

    def issue(r, carry):
        for j in range(TOP_K):
            row_copy(h_ref, r, dest_ref[(i * tm + r) * TOP_K + j], sem).start()
        return carry

    lax.fori_loop(0, tm, issue, 0, unroll=DMA_ISSUE_UNROLL // 2)

    @pl.when(i == pl.num_programs(0) - 1)
    def _():
        zero_ref[...] = jnp.zeros_like(zero_ref)
        n_experts = fill_ref.shape[0] // 2

        def fill_expert(e, carry):
            first, count = fill_ref[2 * e], fill_ref[2 * e + 1]

            def start(r, c):
                row_copy(zero_ref, 0, first + r, zsem).start()
                return c

            def wait(r, c):
                row_copy(zero_ref, 0, first + r, zsem).wait()
                return c

            lax.fori_loop(0, count, start, 0)
            lax.fori_loop(0, count, wait, 0)
            return carry

        lax.fori_loop(0, n_experts, fill_expert, 0)

        def block_copy(b):
            return pltpu.make_async_copy(zero_ref, xs_hbm.at[pl.ds(b * tm, tm), :], zsem)

        def start_block(b, c):
            block_copy(b).start()
            return c

        def wait_block(b, c):
            block_copy(b).wait()
            return c

        lax.fori_loop(nact_ref[0], n_blocks, start_block, 0)
        lax.fori_loop(nact_ref[0], n_blocks, wait_block, 0)

    for j in range(TOP_K):
        pltpu.make_async_copy(h_ref, xs_hbm.at[pl.ds(0, tm), :], sem).wait()


def dispatch_rows(h, dest, fill, n_active, n_blocks, tm=ROW_TILE):
    t, d = h.shape
    grid_spec = pltpu.PrefetchScalarGridSpec(
        num_scalar_prefetch=3,
        grid=(t // tm,),
        in_specs=[pl.BlockSpec((tm, d), lambda i, *_: (i, 0))],
        out_specs=pl.BlockSpec(memory_space=pl.ANY),
        scratch_shapes=[pltpu.VMEM((tm, d), h.dtype), pltpu.SemaphoreType.DMA, pltpu.SemaphoreType.DMA],
    )
    return pl.pallas_call(
        functools.partial(_dispatch_kernel, n_blocks=n_blocks),
        grid_spec=grid_spec,
        out_shape=jax.ShapeDtypeStruct((n_blocks * tm, d), h.dtype),
        compiler_params=_params("arbitrary"),
        name="moe_dispatch",
    )(dest, fill, n_active, h)


def _expert_changed(i, be_ref):
    return (i == 0) | (be_ref[i] != be_ref[jnp.maximum(i - 1, 0)])


def expert_runs(counts, block_expert):
    ne = counts.shape[0]
    ids = jnp.arange(ne, dtype=jnp.int32)
    present = counts > 0
    run_of_expert = jnp.cumsum(present.astype(jnp.int32)) - 1
    later = jnp.where((ids[None, :] > ids[:, None]) & present[None, :], ids[None, :], ne)
    first_present = jnp.min(jnp.where(present, ids, ne))
    nxt = jnp.min(later, axis=1)
    next_of_expert = jnp.where(nxt == ne, first_present, nxt).astype(jnp.int32)
    n_runs = jnp.sum(present.astype(jnp.int32)).reshape(1)
    return run_of_expert[block_expert], next_of_expert[block_expert], n_runs


def _stream_weights(c, i, be_ref, run_ref, next_ref, nruns_ref, copies, casts):
    n_runs = nruns_ref[0]
    g = c * n_runs + run_ref[i]
    slot = g % 2

    @pl.when(g == 0)
    def _():
        for cp in copies(be_ref[i], c, slot):
            cp.start()

    for cp in copies(be_ref[i], c, slot):
        cp.wait()

    @pl.when(g + 1 < pl.num_programs(0) * n_runs)
    def _():
        wraps = run_ref[i] == n_runs - 1
        for cp in copies(next_ref[i], jnp.where(wraps, c + 1, c), 1 - slot):
            cp.start()

    casts(slot)


def _gmm_up_kernel(be_ref, run_ref, next_ref, nruns_ref, nact_ref, x_ref, w_hbm, bg_ref, bu_ref, o_ref,
                   wg_f32, wu_f32, wg_bf, wu_bf, sem, *, layer, tn, dff):
    c = pl.program_id(0)
    i = pl.program_id(1)

    def copies(e, cc, slot):
        col = pl.multiple_of(cc * tn, tn)
        return [pltpu.make_async_copy(w_hbm.at[layer, e, :, pl.ds(col, tn)], wg_f32.at[slot], sem.at[slot]),
                pltpu.make_async_copy(w_hbm.at[layer, e, :, pl.ds(dff + col, tn)], wu_f32.at[slot],
                                      sem.at[slot])]

    def casts(slot):
        wg_bf[...] = wg_f32[slot].astype(BF16)
        wu_bf[...] = wu_f32[slot].astype(BF16)

    @pl.when(i < nact_ref[0])
    def _():
        @pl.when(_expert_changed(i, be_ref))
        def _():
            _stream_weights(c, i, be_ref, run_ref, next_ref, nruns_ref, copies, casts)

        x = _unpack_bf16_halves(x_ref[...])
        g = jnp.dot(x, wg_bf[...], preferred_element_type=F32) + bg_ref[...]
        u = jnp.dot(x, wu_bf[...], preferred_element_type=F32) + bu_ref[...]
        gate = jnp.minimum(g, SWIGLU_LIMIT)
        up = jnp.clip(u, -SWIGLU_LIMIT, SWIGLU_LIMIT)
        act = (up + 1.0) * gate * _sigmoid(gate * SWIGLU_ALPHA)
        o_ref[...] = act.astype(o_ref.dtype)

    @pl.when(i >= nact_ref[0])
    def _():
        o_ref[...] = jnp.zeros_like(o_ref)


def _row_block(i, na):
    return jnp.minimum(i, na[0] - 1)


def gmm_up(xs, w_gu, b_gu, block_expert, runs, n_active, layer, tm=ROW_TILE, tn=1024):
    p = xs.shape[0]
    d = w_gu.shape[-2]
    dff = w_gu.shape[-1] // 2
    tn = min(tn, dff)
    nb, nc = p // tm, dff // tn
    depth, ne = b_gu.shape[:2]
    b4 = b_gu.reshape(depth, ne, 1, 2 * dff)
    grid_spec = pltpu.PrefetchScalarGridSpec(
        num_scalar_prefetch=5,
        grid=(nc, nb),
        in_specs=[
            pl.BlockSpec((tm, d // 2), lambda c, i, be, rn, nx, nr, na: (_row_block(i, na), 0)),
            pl.BlockSpec(memory_space=pl.ANY),
            pl.BlockSpec((None, None, 1, tn),
                         lambda c, i, be, rn, nx, nr, na: (layer, be[_row_block(i, na)], 0, c)),
            pl.BlockSpec((None, None, 1, tn),
                         lambda c, i, be, rn, nx, nr, na: (layer, be[_row_block(i, na)], 0, nc + c)),
        ],
        out_specs=pl.BlockSpec((tm, tn), lambda c, i, *_: (i, c)),
        scratch_shapes=[pltpu.VMEM((2, d, tn), F32), pltpu.VMEM((2, d, tn), F32),
                        pltpu.VMEM((d, tn), BF16), pltpu.VMEM((d, tn), BF16),
                        pltpu.SemaphoreType.DMA((2,))],
    )
    return pl.pallas_call(
        functools.partial(_gmm_up_kernel, layer=layer, tn=tn, dff=dff),
        grid_spec=grid_spec,
        out_shape=jax.ShapeDtypeStruct((p, dff), BF16),
        compiler_params=_params("arbitrary", "arbitrary"),
        name="moe_up",
    )(block_expert, *runs, n_active, xs, w_gu, b4, b4)


def _gmm_down_kernel(be_ref, run_ref, next_ref, nruns_ref, nact_ref, a_ref, w_hbm, b_ref, o_ref,
                     w_f32, w_bf, sem, *, layer, tn):
    c = pl.program_id(0)
    i = pl.program_id(1)

    def copies(e, cc, slot):
        col = pl.multiple_of(cc * tn, tn)
        return [pltpu.make_async_copy(w_hbm.at[layer, e, :, pl.ds(col, tn)], w_f32.at[slot], sem.at[slot])]

    def casts(slot):
        w_bf[...] = w_f32[slot].astype(BF16)

    @pl.when(i < nact_ref[0])
    def _():
        @pl.when(_expert_changed(i, be_ref))
        def _():
            _stream_weights(c, i, be_ref, run_ref, next_ref, nruns_ref, copies, casts)

        o_ref[...] = jnp.dot(a_ref[...], w_bf[...], preferred_element_type=F32) + b_ref[...]

    @pl.when(i >= nact_ref[0])
    def _():
        o_ref[...] = jnp.zeros_like(o_ref)


def gmm_down(act, w_dn, b_dn, block_expert, runs, n_active, layer, tm=ROW_TILE, tn=2048):
    p, dff = act.shape
    d = w_dn.shape[-1]
    tn = min(tn, d)
    nb, nc = p // tm, d // tn
    depth, ne = b_dn.shape[:2]
    b4 = b_dn.reshape(depth, ne, 1, d)
    grid_spec = pltpu.PrefetchScalarGridSpec(
        num_scalar_prefetch=5,
        grid=(nc, nb),
        in_specs=[
            pl.BlockSpec((tm, dff), lambda c, i, be, rn, nx, nr, na: (_row_block(i, na), 0)),
            pl.BlockSpec(memory_space=pl.ANY),
            pl.BlockSpec((None, None, 1, tn),
                         lambda c, i, be, rn, nx, nr, na: (layer, be[_row_block(i, na)], 0, c)),
        ],
        out_specs=pl.BlockSpec((tm, tn), lambda c, i, *_: (i, c)),
        scratch_shapes=[pltpu.VMEM((2, dff, tn), F32), pltpu.VMEM((dff, tn), BF16),
                        pltpu.SemaphoreType.DMA((2,))],
    )
    return pl.pallas_call(
        functools.partial(_gmm_down_kernel, layer=layer, tn=tn),
        grid_spec=grid_spec,
        out_shape=jax.ShapeDtypeStruct((p, d), F32),
        compiler_params=_params("arbitrary", "arbitrary"),
        name="moe_down",
    )(block_expert, *runs, n_active, act, w_dn, b4)


def _combine_kernel(dest_ref, y_hbm, x_ref, gate_ref, g2_ref, ng_ref, sc_ref, sh_ref, o1_ref, o2_ref,
                    buf_ref, sem, *, first_tiles):
    i = pl.program_id(0)
    tm = x_ref.shape[0]

    def issue_block(blk):
        slot = blk % 2

        def issue(r, carry):
            for j in range(TOP_K):
                src = dest_ref[(blk * tm + r) * TOP_K + j]
                pltpu.make_async_copy(y_hbm.at[pl.ds(src, 1), :],
                                      buf_ref.at[slot, j, pl.ds(r, 1), :], sem.at[slot]).start()
            return carry

        lax.fori_loop(0, tm, issue, 0, unroll=DMA_ISSUE_UNROLL // 2)

    @pl.when(i == 0)
    def _():
        issue_block(i)

    @pl.when(i + 1 < pl.num_programs(0))
    def _():
        issue_block(i + 1)

    slot = i % 2
    for j in range(TOP_K):
        pltpu.make_async_copy(y_hbm.at[pl.ds(0, tm), :], buf_ref.at[slot, j], sem.at[slot]).wait()
    gates = gate_ref[...]
    acc = gates[:, 0:1] * buf_ref[slot, 0]
    for j in range(1, TOP_K):
        acc += gates[:, j:j + 1] * buf_ref[slot, j]
    x_new = x_ref[...] + g2_ref[...] * acc
    normed = _rms(x_new) * ng_ref[...]
    if first_tiles is None:
        o1_ref[...] = x_new
        o2_ref[...] = (normed * (1.0 + sc_ref[...]) + sh_ref[...]).astype(o2_ref.dtype)
    else:
        @pl.when(i < first_tiles)
        def _():
            o1_ref[...] = normed

        @pl.when(i >= first_tiles)
        def _():
            o2_ref[...] = normed


def combine_residual(y, dest, gates, x, mods, layer, mod_of_tile, next_gain, n_first, tm=ROW_TILE):
    t, d = x.shape
    last = n_first is not None
    tile = lambda i, dst: (i, 0)
    if last:
        first_tiles = n_first // tm
        out_specs = [pl.BlockSpec((tm, d), lambda i, dst: (jnp.minimum(i, first_tiles - 1), 0)),
                     pl.BlockSpec((tm, d), lambda i, dst: (jnp.maximum(i - first_tiles, 0), 0))]
        out_shape = [jax.ShapeDtypeStruct((n_first, d), F32), jax.ShapeDtypeStruct((t - n_first, d), F32)]
        gain_spec = pl.BlockSpec((None, 1, d), lambda i, dst: (0, 0, 0))
        mod_layer = layer
    else:
        first_tiles = None
        out_specs = [pl.BlockSpec((tm, d), tile), pl.BlockSpec((tm, d), tile)]
        out_shape = [jax.ShapeDtypeStruct((t, d), F32), jax.ShapeDtypeStruct((t, d), BF16)]
        gain_spec = pl.BlockSpec((None, 1, d), lambda i, dst: (layer + 1, 0, 0))
        mod_layer = layer + 1
    grid_spec = pltpu.PrefetchScalarGridSpec(
        num_scalar_prefetch=1,
        grid=(t // tm,),
        in_specs=[
            pl.BlockSpec(memory_space=pl.ANY),
            pl.BlockSpec((tm, d), tile),
            pl.BlockSpec((tm, LANES), tile),
            _mod_spec(d, layer, 5, mod_of_tile),
            gain_spec,
            _mod_spec(d, mod_layer, 1, mod_of_tile),
            _mod_spec(d, mod_layer, 0, mod_of_tile),
        ],
        out_specs=out_specs,
        scratch_shapes=[pltpu.VMEM((2, TOP_K, tm, d), F32), pltpu.SemaphoreType.DMA((2,))],
    )
    return pl.pallas_call(
        functools.partial(_combine_kernel, first_tiles=first_tiles),
        grid_spec=grid_spec,
        out_shape=out_shape,
        compiler_params=_params("arbitrary"),
        name="moe_combine",
    )(dest, y, x, gates, mods, next_gain, mods, mods)


def moe_layer(x, gain, mods, router_w, router_b, w_gu, b_gu, w_dn, b_dn, layer, mod_of_tile,
              next_gain, n_first):
    t, d = x.shape
    ne = router_w.shape[-1]
    n_blocks = t * TOP_K // ROW_TILE + ne
    h, idx, gates, rank, counts = route(x, gain, mods, router_w, router_b, layer, mod_of_tile)
    counts = counts[0, :ne]
    padded = (counts + ROW_TILE - 1) // ROW_TILE * ROW_TILE
    ends = jnp.cumsum(padded)
    pstart = ends - padded
    e_idx = idx[:, :TOP_K]
    dest = (pstart[e_idx] + rank[:, :TOP_K]).reshape(t * TOP_K)
    block_start = jnp.arange(n_blocks, dtype=jnp.int32) * ROW_TILE
    block_expert = jnp.minimum(jnp.sum((ends[None, :] <= block_start[:, None]).astype(jnp.int32), axis=1),
                               ne - 1)
    n_active = (ends[-1:] // ROW_TILE).astype(jnp.int32)
    fill = jnp.stack([pstart + counts, padded - counts], axis=1).reshape(2 * ne).astype(jnp.int32)
    runs = expert_runs(counts, block_expert)
    xs = dispatch_rows(h, dest, fill, n_active, n_blocks)
    act = gmm_up(xs, w_gu, b_gu, block_expert, runs, n_active, layer)
    y = gmm_down(act, w_dn, b_dn, block_expert, runs, n_active, layer)
    return combine_residual(y, dest, gates, x, mods, layer, mod_of_tile, next_gain, n_first)


ATTN_SCALE = HEAD_DIM ** -0.5
_NT = (((1,), (1,)), ((), ()))
_TN = (((0,), (0,)), ((), ()))


def _head(h):
    return slice(h * HEAD_DIM, (h + 1) * HEAD_DIM)


def _ctx_attn_kernel(q_ref, k_ref, v_ref, o_ref):
    for h in range(N_HEADS):
        q = q_ref[:, _head(h)].astype(BF16)
        k = k_ref[:, _head(h)].astype(BF16)
        v = v_ref[:, _head(h)].astype(BF16)
        s = lax.dot_general(q, k, _NT, preferred_element_type=F32) * ATTN_SCALE
        e = jnp.exp(s - jnp.max(s, axis=-1, keepdims=True))
        p = e / jnp.sum(e, axis=-1, keepdims=True)
        o_ref[:, _head(h)] = jnp.dot(p.astype(BF16), v, preferred_element_type=F32).astype(o_ref.dtype)


def context_attention(proj, n_seq, seq_len):
    def spec(group):
        return pl.BlockSpec((seq_len, MIX_W), lambda b: (b, group))

    return pl.pallas_call(
        _ctx_attn_kernel,
        grid=(n_seq,),
        in_specs=[spec(0), spec(1), spec(2)],
        out_specs=pl.BlockSpec((seq_len, MIX_W), lambda b: (b, 0)),
        out_shape=jax.ShapeDtypeStruct((n_seq * seq_len, MIX_W), BF16),
        compiler_params=_params("arbitrary"),
        name="ctx_attn",
    )(proj, proj, proj)


def na_bias_table(rpb):
    depth, nh = rpb.shape[:2]
    qc = np.arange(GRID_W)[:, None]
    kc = np.arange(GRID_W)[None, :]
    d_col = np.clip(kc - qc + NA_WIN_COLS - 1, 0, 2 * NA_WIN_COLS - 2)
    c0 = np.clip(qc - NA_WIN_COLS // 2, 0, GRID_W - NA_WIN_COLS)
    inside = (kc >= c0) & (kc < c0 + NA_WIN_COLS)
    onehot = (d_col.reshape(-1)[None, :] == np.arange(2 * NA_WIN_COLS - 1)[:, None]).astype(np.float32)
    tt = jnp.einsum('lhij,jm->lhim', rpb, jnp.asarray(onehot), precision=lax.Precision.HIGHEST)
    tt = jnp.where(inside[None, None, None], tt.reshape(depth, nh, -1, GRID_W, GRID_W), NEG_BIG)
    return jnp.stack([jnp.concatenate([tt[:, :, i0 + kk] for kk in range(NA_WIN_ROWS)], axis=-1)
                      for i0 in range(NA_WIN_ROWS)], axis=2)


def _na_attn_kernel(q_ref, k_ref, v_ref, ck_ref, cv_ref, bias_ref, o_ref, *, rows):
    r = pl.program_id(1)
    r0 = jnp.clip(r - NA_WIN_ROWS // 2, 0, rows - NA_WIN_ROWS)
    start = pl.multiple_of(r0 * GRID_W, GRID_W)
    win = NA_WIN_ROWS * GRID_W
    for h in range(N_HEADS):
        q = q_ref[:, _head(h)].astype(BF16)
        kw = k_ref[pl.ds(start, win), _head(h)].astype(BF16)
        vw = v_ref[pl.ds(start, win), _head(h)].astype(BF16)
        ck = ck_ref[:, _head(h)].astype(BF16)
        cv = cv_ref[:, _head(h)].astype(BF16)
        s_loc = lax.dot_general(q, kw, _NT, preferred_element_type=F32) * ATTN_SCALE + bias_ref[h]
        s_ctx = lax.dot_general(q, ck, _NT, preferred_element_type=F32) * ATTN_SCALE
        m = jnp.maximum(jnp.max(s_loc, axis=-1, keepdims=True), jnp.max(s_ctx, axis=-1, keepdims=True))
        e_loc = jnp.exp(s_loc - m)
        e_ctx = jnp.exp(s_ctx - m)
        den = jnp.sum(e_loc, axis=-1, keepdims=True) + jnp.sum(e_ctx, axis=-1, keepdims=True)
        o = (jnp.dot((e_loc / den).astype(BF16), vw, preferred_element_type=F32)
             + jnp.dot((e_ctx / den).astype(BF16), cv, preferred_element_type=F32))
        o_ref[:, _head(h)] = o.astype(o_ref.dtype)


def neighbourhood_attention(proj, cache_k, cache_v, bias_table, layer, row0, n_seq, seq_len):
    rows = seq_len // GRID_W
    assert rows >= NA_WIN_ROWS and row0 % seq_len == 0
    ctx = cache_k.shape[2]
    ck, cv = cache_k, cache_v

    def offset_in_window(r):
        return jnp.clip(r - NA_WIN_ROWS // 2, 0, rows - NA_WIN_ROWS) - r + NA_WIN_ROWS - 1

    def seq_spec(group):
        return pl.BlockSpec((seq_len, MIX_W), lambda b, r: (row0 // seq_len + b, group))

    ctx_spec = pl.BlockSpec((None, None, ctx, MIX_W), lambda b, r: (b, layer, 0, 0))
    return pl.pallas_call(
        functools.partial(_na_attn_kernel, rows=rows),
        grid=(n_seq, rows),
        in_specs=[
            pl.BlockSpec((GRID_W, MIX_W), lambda b, r: (row0 // GRID_W + b * rows + r, 0)),
            seq_spec(1), seq_spec(2), ctx_spec, ctx_spec,
            pl.BlockSpec((None, N_HEADS, None, GRID_W, NA_WIN_ROWS * GRID_W),
                         lambda b, r: (layer, 0, offset_in_window(r), 0, 0)),
        ],
        out_specs=pl.BlockSpec((GRID_W, MIX_W), lambda b, r: (b * rows + r, 0)),
        out_shape=jax.ShapeDtypeStruct((n_seq * seq_len, MIX_W), BF16),
        compiler_params=_params("arbitrary", "arbitrary"),
        name="na_attn",
    )(proj, proj, proj, ck, cv, bias_table)


def _gelu_tanh(x):
    return 0.5 * x * (1.0 + jnp.tanh(math.sqrt(2.0 / math.pi) * (x + 0.044715 * (x * x * x))))


def _sgu_kernel(u_ref, v_ref, w_ref, b_ref, o_ref):
    for g in range(N_HEADS):
        u = _gelu_tanh(u_ref[:, _head(g)])
        v = _gelu_tanh(v_ref[:, _head(g)])
        vn = (v * lax.rsqrt(jnp.mean(v * v, axis=-1, keepdims=True) + EPS)).astype(BF16)
        mixed = jnp.dot(w_ref[g].astype(BF16), vn, preferred_element_type=F32) + b_ref[:, _head(g)]
        o_ref[:, _head(g)] = (u * mixed).astype(o_ref.dtype)


def spatial_gating(proj, sgu_w, sgu_b, layer):
    t = proj.shape[0]
    ch = sgu_w.shape[-1]
    depth = sgu_w.shape[0]
    bias = jnp.repeat(jnp.swapaxes(sgu_b, 1, 2), HEAD_DIM, axis=2)
    return pl.pallas_call(
        _sgu_kernel,
        grid=(t // ch,),
        in_specs=[
            pl.BlockSpec((ch, MIX_W), lambda i: (i, 3)),
            pl.BlockSpec((ch, MIX_W), lambda i: (i, 4)),
            pl.BlockSpec((None, N_HEADS, ch, ch), lambda i: (layer, 0, 0, 0)),
            pl.BlockSpec((None, ch, MIX_W), lambda i: (layer, 0, 0)),
        ],
        out_specs=pl.BlockSpec((ch, MIX_W), lambda i: (i, 0)),
        out_shape=jax.ShapeDtypeStruct((t, MIX_W), BF16),
        compiler_params=_params("arbitrary"),
        name="sgu",
    )(proj, proj, sgu_w, bias)


def _pool_kernel(x_ref, w_ref, s_ref, o_ref):
    n = x_ref.shape[0]
    t = lax.broadcasted_iota(jnp.int32, (n, n), 0)
    s = lax.broadcasted_iota(jnp.int32, (n, n), 1)
    tc = lax.broadcasted_iota(jnp.int32, (n, 1), 0)
    for g, win in enumerate(POOL_WINDOWS):
        half = win // 2
        band = jnp.where(s >= t - half, jnp.where(s < t + half, 1.0, 0.0), 0.0).astype(BF16)
        cnt = (jnp.minimum(tc + half, n) - jnp.maximum(tc - half, 0)).astype(F32)
        x = x_ref[:, _head(g)]
        hi = x.astype(BF16)
        lo = (x - hi.astype(F32)).astype(BF16)
        tot = jnp.dot(band, hi, preferred_element_type=F32) + jnp.dot(band, lo, preferred_element_type=F32)
        pooled = tot / cnt - x
        y = jnp.dot(pooled.astype(BF16), w_ref[g].astype(BF16), preferred_element_type=F32)
        o_ref[:, _head(g)] = (y * s_ref[:, _head(g)]).astype(o_ref.dtype)


def multiscale_pool(proj, pool_w, pool_scale, layer, row0, n_seq, seq_len):
    depth = pool_w.shape[0]
    return pl.pallas_call(
        _pool_kernel,
        grid=(n_seq,),
        in_specs=[
            pl.BlockSpec((seq_len, MIX_W), lambda b: (row0 // seq_len + b, N_PROJ - 1)),
            pl.BlockSpec((None, N_HEADS, HEAD_DIM, HEAD_DIM), lambda b: (layer, 0, 0, 0)),
            pl.BlockSpec((None, 1, MIX_W), lambda b: (layer, 0, 0)),
        ],
        out_specs=pl.BlockSpec((seq_len, MIX_W), lambda b: (b, 0)),
        out_shape=jax.ShapeDtypeStruct((n_seq * seq_len, MIX_W), BF16),
        compiler_params=_params("arbitrary"),
        name="pool",
    )(proj, pool_w, pool_scale.reshape(depth, 1, MIX_W))


HGRN_BLOCK = 128
HGRN_DIAG = 8


def _log_forget_k(z, lb):
    a = jnp.log(lb + LB_TINY)
    b = jnp.log1p(-lb) + (jnp.minimum(z, 0.0) - jnp.log1p(jnp.exp(-jnp.abs(z))))
    return jnp.maximum(a, b) + jnp.log1p(jnp.exp(-jnp.abs(a - b)))


def _rows_of(x, idx, span):
    return jnp.concatenate([jnp.broadcast_to(x[i:i + 1, :], (span, x.shape[1])) for i in idx], axis=0)


def _hgrn_block(q, k, v, logf, s0, rev):
    n = HGRN_BLOCK
    row = lax.broadcasted_iota(jnp.int32, (n, n), 0)
    col = lax.broadcasted_iota(jnp.int32, (n, n), 1)
    cum = logf
    sh = 1
    while sh < n:
        if rev:
            cum = cum + jnp.where(row < n - sh, pltpu.roll(cum, n - sh, 0), 0.0)
        else:
            cum = cum + jnp.where(row >= sh, pltpu.roll(cum, sh, 0), 0.0)
        sh *= 2
    att = jnp.zeros((n, n), F32)
    half = n // 2
    while half >= HGRN_DIAG:
        span = 2 * half
        pos = row % span
        first = (pos >= half) if rev else (pos < half)
        edge = half if rev else half - 1
        ref = _rows_of(cum, [j * span + edge for j in range(n // span)], span)
        qe = jnp.where(first, 0.0, q * jnp.exp(jnp.minimum(cum - ref, 0.0)))
        ke = jnp.where(first, k * jnp.exp(jnp.minimum(ref - cum, 0.0)), 0.0)
        a = lax.dot_general(qe.astype(BF16), ke.astype(BF16), _NT, preferred_element_type=F32)
        att = att + jnp.where(row // span == col // span, a, 0.0)
        half //= 2
    pos = row % HGRN_DIAG
    group_col = row - pos

    def row_of_group(x, j):
        x3 = x.reshape(n // HGRN_DIAG, HGRN_DIAG, n)
        return jnp.broadcast_to(x3[:, j:j + 1, :], x3.shape).reshape(n, n)

    for j in range(HGRN_DIAG):
        a = q * row_of_group(k, j) * jnp.exp(jnp.minimum(cum - row_of_group(cum, j), 0.0))
        a = jnp.where((pos <= j) if rev else (pos >= j), a, 0.0)
        w = jnp.sum(a, axis=-1, keepdims=True)
        att = att + jnp.where(col == group_col + j, w, 0.0)
    o = jnp.dot(att.astype(BF16), v.astype(BF16), preferred_element_type=F32)
    o = o + jnp.dot((q * jnp.exp(cum)).astype(BF16), s0.astype(BF16), preferred_element_type=F32)
    last = cum[0:1, :] if rev else cum[n - 1:n, :]
    kd = (k * jnp.exp(last - cum)).astype(BF16)
    upd = lax.dot_general(kd, v.astype(BF16), _TN, preferred_element_type=F32)
    keep = jnp.sum(jnp.where(row == col, jnp.exp(last), 0.0), axis=-1, keepdims=True)
    return o, keep * s0 + upd


def _hgrn_kernel(qf_ref, if_ref, ff_ref, qb_ref, ib_ref, fb_ref, lbf_ref, lbb_ref, s0_ref,
                 of_ref, ob_ref, sfin_ref, state_ref, *, step_info):
    from_zero, c, nblk = step_info(pl.program_id(0))[:3]

    @pl.when((c == 0) & from_zero)
    def _():
        state_ref[...] = jnp.zeros_like(state_ref)

    @pl.when((c == 0) & jnp.logical_not(from_zero))
    def _():
        state_ref[...] = s0_ref[...]

    for rev, (q_ref, i_ref, f_ref, lb_ref, o_ref) in enumerate(
            ((qf_ref, if_ref, ff_ref, lbf_ref, of_ref), (qb_ref, ib_ref, fb_ref, lbb_ref, ob_ref))):
        for h in range(N_HEADS):
            q = q_ref[:, _head(h)]
            q = q * _sigmoid(q)
            logf = _log_forget_k(f_ref[:, _head(h)], lb_ref[:, _head(h)])
            k = 1.0 - jnp.exp(logf)
            o, s1 = _hgrn_block(q, k, i_ref[:, _head(h)], logf, state_ref[rev, h], bool(rev))
            o_ref[:, _head(h)] = o
            state_ref[rev, h] = s1

    @pl.when((c == nblk - 1) & from_zero)
    def _():
        sfin_ref[...] = state_ref[...]


def hgrn2_scan(proj, lb_f, lb_b, state_in, layer, n_zero, len_zero, n_init, len_init):
    blk = HGRN_BLOCK
    nz, ni = len_zero // blk, len_init // blk
    steps_zero = n_zero * nz

    def step_info(s):
        from_zero = s < steps_zero
        s2 = s - steps_zero
        b = jnp.where(from_zero, s // nz, s2 // ni)
        c = jnp.where(from_zero, s % nz, s2 % ni)
        nblk = jnp.where(from_zero, nz, ni)
        base = jnp.where(from_zero, b * nz, steps_zero + b * ni)
        return from_zero, c, nblk, b, base

    def fwd(group):
        def index(s):
            _, c, _, _, base = step_info(s)
            return (base + c, group)
        return pl.BlockSpec((blk, MIX_W), index)

    def bwd(group):
        def index(s):
            _, c, nblk, _, base = step_info(s)
            return (base + nblk - 1 - c, group)
        return pl.BlockSpec((blk, MIX_W), index)

    def init_index(s):
        from_zero, _, _, b, _ = step_info(s)
        return (jnp.where(from_zero, 0, b), layer, 0, 0, 0, 0)

    def final_index(s):
        from_zero, _, _, b, _ = step_info(s)
        return (jnp.where(from_zero, b, n_zero - 1), 0, 0, 0, 0)

    t = proj.shape[0]
    state_block = (None, 2, N_HEADS, HEAD_DIM, HEAD_DIM)
    lb_spec = pl.BlockSpec((1, MIX_W), lambda s: (0, 0))
    return pl.pallas_call(
        functools.partial(_hgrn_kernel, step_info=step_info),
        grid=(steps_zero + n_init * ni,),
        in_specs=[fwd(5), fwd(6), fwd(7), bwd(5), bwd(6), bwd(8), lb_spec, lb_spec,
                  pl.BlockSpec((None, None, 2, N_HEADS, HEAD_DIM, HEAD_DIM), init_index)],
        out_specs=[fwd(0), bwd(0), pl.BlockSpec(state_block, final_index)],
        out_shape=[jax.ShapeDtypeStruct((t, MIX_W), F32), jax.ShapeDtypeStruct((t, MIX_W), F32),
                   jax.ShapeDtypeStruct((n_zero, 2, N_HEADS, HEAD_DIM, HEAD_DIM), F32)],
        scratch_shapes=[pltpu.VMEM((2, N_HEADS, HEAD_DIM, HEAD_DIM), F32)],
        compiler_params=_params("arbitrary"),
        name="hgrn_scan",
    )(proj, proj, proj, proj, proj, proj, lb_f.reshape(1, MIX_W), lb_b.reshape(1, MIX_W), state_in)


def _hgrn_out_kernel(of_ref, ob_ref, g_ref, ng_ref, o_ref):
    for h in range(N_HEADS):
        o = of_ref[:, _head(h)] + ob_ref[:, _head(h)]
        o = o * lax.rsqrt(jnp.mean(o * o, axis=-1, keepdims=True) + EPS)
        g = g_ref[:, _head(h)]
        o_ref[:, _head(h)] = (o * ng_ref[:, _head(h)] * (g * _sigmoid(g))).astype(o_ref.dtype)


def hgrn2_output(o_f, o_b, proj, norm_g, layer, tm=512):
    t = o_f.shape[0]
    depth = norm_g.shape[0]
    tile = pl.BlockSpec((tm, MIX_W), lambda i: (i, 0))
    return pl.pallas_call(
        _hgrn_out_kernel,
        grid=(t // tm,),
        in_specs=[tile, tile, pl.BlockSpec((tm, MIX_W), lambda i: (i, 9)),
                  pl.BlockSpec((None, 1, MIX_W), lambda i: (layer, 0, 0))],
        out_specs=tile,
        out_shape=jax.ShapeDtypeStruct((t, MIX_W), BF16),
        compiler_params=_params("arbitrary"),
        name="hgrn_out",
    )(o_f, o_b, proj, norm_g.reshape(depth, 1, MIX_W))


def kernel(x_prompt, x_sample, cache_k, cache_v, state_hgrn, c, c_ctx, w_mod, b_mod, norm1_g, norm2_g, w_in, na_rpb, sgu_w, sgu_b, hgrn_lb, hgrn_norm_g, pool_w, pool_scale, w_out, router_w, router_b, exp_w_gu, exp_b_gu, exp_w_dn, exp_b_dn, final_norm_g):
    bp, sp, d = x_prompt.shape
    bs, ss, _ = x_sample.shape
    depth = w_mod.shape[0]
    tp, ts = bp * sp, bs * ss

    cond = jnp.zeros((8, d), F32).at[0].set(c_ctx).at[1:1 + bs].set(c)
    mods = adaln_all(cond, w_mod, b_mod).reshape(depth * 8 * N_MOD, 1, d)

    def mod_of_tile(tile_rows):
        prompt_tiles = tp // tile_rows
        per_seq = ss // tile_rows
        return lambda i: jnp.where(i < prompt_tiles, 0, 1 + (i - prompt_tiles) // per_seq)

    lb_p = jax.nn.softmax(hgrn_lb, axis=1)
    lb_all = jnp.maximum(jnp.cumsum(lb_p, axis=1) - lb_p[:, :1], 0.0)

    x = jnp.concatenate([x_prompt.reshape(tp, d), x_sample.reshape(ts, d)], axis=0)
    g1 = norm1_g.reshape(depth, 1, d)
    g2 = norm2_g.reshape(depth, 1, d)
    bias_table = na_bias_table(na_rpb)
    cache_k = cache_k.reshape(bs, depth, cache_k.shape[2], MIX_W)
    cache_v = cache_v.reshape(bs, depth, cache_v.shape[2], MIX_W)
    new_k, new_v, new_s = [], [], []
    h = norm_mod(x, g1, mods, 0, 1, 0, mod_of_tile(ROW_TILE))
    for l in range(depth):
        last = l == depth - 1
        proj = proj_in(h, w_in, l)
        o_a = jnp.concatenate([context_attention(proj, bp, sp),
                               neighbourhood_attention(proj, cache_k, cache_v, bias_table, l, tp, bs, ss)])
        o_b = spatial_gating(proj, sgu_w, sgu_b, l)
        o_f, o_r, s_l = hgrn2_scan(proj, lb_all[0, l], lb_all[1, l], state_hgrn, l, bp, sp, bs, ss)
        o_c = hgrn2_output(o_f, o_r, proj, hgrn_norm_g, l)
        o_d = jnp.concatenate([multiscale_pool(proj, pool_w, pool_scale, l, 0, bp, sp),
                               multiscale_pool(proj, pool_w, pool_scale, l, tp, bs, ss)])
        new_k.append(proj[:tp, MIX_W:2 * MIX_W].reshape(bp, sp, N_HEADS, HEAD_DIM))
        new_v.append(proj[:tp, 2 * MIX_W:3 * MIX_W].reshape(bp, sp, N_HEADS, HEAD_DIM))
        new_s.append(s_l)
        x = proj_out_residual([o_a, o_b, o_c, o_d], w_out, x, mods, l, mod_of_tile(1024))
        x, h = moe_layer(x, g2, mods, router_w, router_b, exp_w_gu, exp_b_gu, exp_w_dn, exp_b_dn, l,
                         mod_of_tile(ROW_TILE), final_norm_g.reshape(1, 1, d) if last else g1,
                         tp if last else None)
    return (x.reshape(bp, sp, d), h.reshape(bs, ss, d),
            jnp.stack(new_k, axis=1), jnp.stack(new_v, axis=1), jnp.stack(new_s, axis=1))
```

```python
import functools
import math

import numpy as np
import jax
import jax.numpy as jnp
from jax import lax
from jax.experimental import pallas as pl
from jax.experimental.pallas import tpu as pltpu

F32 = jnp.float32
BF16 = jnp.bfloat16

N_MOD = 6
N_PROJ = 11
MIX_W = 512
HEAD_DIM = 128
N_HEADS = 4
GRID_W = 64
NA_WIN_ROWS = 8
NA_WIN_COLS = 16
POOL_WINDOWS = (2, 4, 8, 16)
N_EXPERTS = 32
TOP_K = 4
SWIGLU_ALPHA = 1.702
SWIGLU_LIMIT = 7.0
EPS = 1e-6
LB_TINY = 1e-30
NEG_BIG = -1e30
HGRN_CHUNK = 16

LANES = 128
ROW_TILE = 256
VMEM_LIMIT = 56 * 1024 * 1024


def _params(*sem, vmem=VMEM_LIMIT):
    return pltpu.CompilerParams(dimension_semantics=sem, vmem_limit_bytes=vmem)


def _sigmoid(x):
    return 1.0 / (1.0 + jnp.exp(-x))


def _adaln_kernel(c_ref, w_ref, b_ref, o_ref):
    c = c_ref[...]
    s = (c * _sigmoid(c)).astype(BF16)
    o_ref[...] = jnp.dot(s, w_ref[...].astype(BF16), preferred_element_type=F32) + b_ref[...]


def adaln_all(cond, w_mod, b_mod, tn=1024):
    depth, d, n = w_mod.shape
    rows = cond.shape[0]
    tn = min(tn, n)
    return pl.pallas_call(
        _adaln_kernel,
        grid=(depth, n // tn),
        in_specs=[
            pl.BlockSpec((rows, d), lambda l, j: (0, 0)),
            pl.BlockSpec((None, d, tn), lambda l, j: (l, 0, j)),
            pl.BlockSpec((None, 1, tn), lambda l, j: (l, 0, j)),
        ],
        out_specs=pl.BlockSpec((None, rows, tn), lambda l, j: (l, 0, j)),
        out_shape=jax.ShapeDtypeStruct((depth, rows, n), F32),
        compiler_params=_params("arbitrary", "arbitrary"),
        name="adaln",
    )(cond, w_mod, b_mod.reshape(depth, 1, n))


def _rms(x):
    return x * lax.rsqrt(jnp.mean(x * x, axis=-1, keepdims=True) + EPS)


def _normmod_kernel(x_ref, g_ref, sc_ref, sh_ref, o_ref):
    h = (_rms(x_ref[...]) * g_ref[...]) * (1.0 + sc_ref[...]) + sh_ref[...]
    o_ref[...] = h.astype(o_ref.dtype)


def _mod_spec(d, layer, which, mod_of_tile):
    return pl.BlockSpec((None, 1, d), lambda i, *_: ((layer * 8 + mod_of_tile(i)) * N_MOD + which, 0, 0))


def norm_mod(x, gain, mods, layer, which_scale, which_shift, mod_of_tile, tm=ROW_TILE):
    t, d = x.shape
    return pl.pallas_call(
        _normmod_kernel,
        grid=(t // tm,),
        in_specs=[
            pl.BlockSpec((tm, d), lambda i: (i, 0)),
            pl.BlockSpec((None, 1, d), lambda i: (layer, 0, 0)),
            _mod_spec(d, layer, which_scale, mod_of_tile),
            _mod_spec(d, layer, which_shift, mod_of_tile),
        ],
        out_specs=pl.BlockSpec((tm, d), lambda i: (i, 0)),
        out_shape=jax.ShapeDtypeStruct((t, d), BF16),
        compiler_params=_params("arbitrary"),
        name="norm_mod",
    )(x, gain, mods, mods)


def _mm_kernel(x_ref, w_ref, o_ref, wbf_ref):
    @pl.when(pl.program_id(1) == 0)
    def _():
        wbf_ref[...] = w_ref[...].astype(BF16)

    o_ref[...] = jnp.dot(x_ref[...], wbf_ref[...], preferred_element_type=F32).astype(o_ref.dtype)


def proj_in(h, w_in, layer, tm=1024, tn=512):
    t, d = h.shape
    n = w_in.shape[-1]
    tm, tn = min(tm, t), min(tn, n)
    return pl.pallas_call(
        _mm_kernel,
        grid=(n // tn, t // tm),
        in_specs=[
            pl.BlockSpec((tm, d), lambda j, i: (i, 0)),
            pl.BlockSpec((None, d, tn), lambda j, i: (layer, 0, j)),
        ],
        out_specs=pl.BlockSpec((tm, tn), lambda j, i: (i, j)),
        out_shape=jax.ShapeDtypeStruct((t, n), F32),
        scratch_shapes=[pltpu.VMEM((d, tn), BF16)],
        compiler_params=_params("arbitrary", "arbitrary"),
        name="proj_in",
    )(h, w_in)


def _proj_out_kernel(a_ref, b_ref, c_ref, d_ref, w_ref, x_ref, g_ref, o_ref, wbf_ref):
    @pl.when(pl.program_id(1) == 0)
    def _():
        wbf_ref[...] = w_ref[...].astype(BF16)

    k = a_ref.shape[1]
    acc = jnp.dot(a_ref[...], wbf_ref[0:k, :], preferred_element_type=F32)
    acc += jnp.dot(b_ref[...], wbf_ref[k:2 * k, :], preferred_element_type=F32)
    acc += jnp.dot(c_ref[...], wbf_ref[2 * k:3 * k, :], preferred_element_type=F32)
    acc += jnp.dot(d_ref[...], wbf_ref[3 * k:4 * k, :], preferred_element_type=F32)
    o_ref[...] = x_ref[...] + g_ref[...] * acc


def proj_out_residual(parts, w_out, x, mods, layer, mod_of_tile, tm=1024, tn=512):
    t, d = x.shape
    k = parts[0].shape[1]
    tm, tn = min(tm, t), min(tn, d)
    part_spec = pl.BlockSpec((tm, k), lambda j, i: (i, 0))
    return pl.pallas_call(
        _proj_out_kernel,
        grid=(d // tn, t // tm),
        in_specs=[part_spec] * 4 + [
            pl.BlockSpec((None, d, tn), lambda j, i: (layer, 0, j)),
            pl.BlockSpec((tm, tn), lambda j, i: (i, j)),
            pl.BlockSpec((None, 1, tn),
                         lambda j, i: ((layer * 8 + mod_of_tile(i)) * N_MOD + 2, 0, j)),
        ],
        out_specs=pl.BlockSpec((tm, tn), lambda j, i: (i, j)),
        out_shape=jax.ShapeDtypeStruct((t, d), F32),
        scratch_shapes=[pltpu.VMEM((d, tn), BF16)],
        compiler_params=_params("arbitrary", "arbitrary"),
        name="proj_out",
    )(*parts, w_out, x, mods)


def _pack_bf16_halves(x):
    n = x.shape[1] // 2
    lo = lax.bitcast_convert_type(x[:, :n].astype(BF16).astype(F32), jnp.uint32)
    hi = lax.bitcast_convert_type(x[:, n:].astype(BF16).astype(F32), jnp.uint32)
    return (lo >> 16) | (hi & jnp.uint32(0xFFFF0000))


def _unpack_bf16_halves(w):
    lo = lax.bitcast_convert_type(w << 16, F32).astype(BF16)
    hi = lax.bitcast_convert_type(w & jnp.uint32(0xFFFF0000), F32).astype(BF16)
    return jnp.concatenate([lo, hi], axis=1)


def _router_kernel(x_ref, g_ref, sc_ref, sh_ref, rw_ref, rb_ref,
                   h_ref, idx_ref, gate_ref, rank_ref, cnt_ref, carry_ref):
    i = pl.program_id(0)

    @pl.when(i == 0)
    def _():
        carry_ref[...] = jnp.zeros_like(carry_ref)

    h = (_rms(x_ref[...]) * g_ref[...]) * (1.0 + sc_ref[...]) + sh_ref[...]
    packed = _pack_bf16_halves(h)
    s_rows = packed.shape[1] // LANES
    for s in range(s_rows):
        h_ref[pl.ds(s, h.shape[0], stride=s_rows), :] = packed[:, s * LANES:(s + 1) * LANES]
    h_hi = h.astype(BF16)
    h_lo = (h - h_hi.astype(F32)).astype(BF16)
    logits = (jnp.dot(h_hi, rw_ref[0], preferred_element_type=F32)
              + jnp.dot(h_lo, rw_ref[0], preferred_element_type=F32)
              + jnp.dot(h_hi, rw_ref[1], preferred_element_type=F32)) + rb_ref[...]
    tm = logits.shape[0]
    lane = lax.broadcasted_iota(jnp.int32, (tm, LANES), 1)
    vals, hots = [], []
    idx_out = jnp.zeros((tm, LANES), jnp.int32)
    for j in range(TOP_K):
        m = jnp.max(logits, axis=-1, keepdims=True)
        idx = jnp.min(jnp.where(logits == m, lane, LANES), axis=-1, keepdims=True)
        hot = lane == idx
        vals.append(m)
        hots.append(hot)
        idx_out = jnp.where(lane == j, idx, idx_out)
        logits = jnp.where(hot, -jnp.inf, logits)
    exps = [jnp.exp(v - vals[0]) for v in vals]
    denom = exps[0] + exps[1] + exps[2] + exps[3]
    gate_out = jnp.zeros((tm, LANES), F32)
    for j in range(TOP_K):
        gate_out = jnp.where(lane == j, exps[j] / denom, gate_out)
    chosen = (hots[0] | hots[1] | hots[2] | hots[3])
    chosen_f = jnp.where(chosen, 1.0, 0.0)
    row = lax.broadcasted_iota(jnp.int32, (tm, tm), 0)
    col = lax.broadcasted_iota(jnp.int32, (tm, tm), 1)
    before = jnp.where(col < row, 1.0, 0.0).astype(BF16)
    base = carry_ref[...] + jnp.dot(before, chosen_f.astype(BF16), preferred_element_type=F32)
    rank_out = jnp.zeros((tm, LANES), F32)
    for j in range(TOP_K):
        r = jnp.sum(jnp.where(hots[j], base, 0.0), axis=-1, keepdims=True)
        rank_out = jnp.where(lane == j, r, rank_out)
    carry_ref[...] += jnp.sum(chosen_f, axis=0, keepdims=True)
    idx_ref[...] = idx_out
    gate_ref[...] = gate_out
    rank_ref[...] = rank_out.astype(jnp.int32)
    cnt_ref[...] = carry_ref[...].astype(jnp.int32)


def route(x, gain, mods, router_w, router_b, layer, mod_of_tile, tm=ROW_TILE):
    t, d = x.shape
    ne = router_w.shape[-1]
    rw = jnp.pad(router_w[layer], ((0, 0), (0, LANES - ne)))
    rw_hi = rw.astype(BF16)
    rw = jnp.stack([rw_hi, (rw - rw_hi.astype(F32)).astype(BF16)])
    rb = jnp.pad(router_b[layer], (0, LANES - ne), constant_values=-jnp.inf).reshape(1, LANES)
    tile = pl.BlockSpec((tm, LANES), lambda i: (i, 0))
    return pl.pallas_call(
        _router_kernel,
        grid=(t // tm,),
        in_specs=[
            pl.BlockSpec((tm, d), lambda i: (i, 0)),
            pl.BlockSpec((None, 1, d), lambda i: (layer, 0, 0)),
            _mod_spec(d, layer, 4, mod_of_tile),
            _mod_spec(d, layer, 3, mod_of_tile),
            pl.BlockSpec((2, d, LANES), lambda i: (0, 0, 0)),
            pl.BlockSpec((1, LANES), lambda i: (0, 0)),
        ],
        out_specs=[pl.BlockSpec((tm * (d // 2 // LANES), LANES), lambda i: (i, 0)), tile, tile, tile,
                   pl.BlockSpec((1, LANES), lambda i: (0, 0))],
        out_shape=[jax.ShapeDtypeStruct((t * (d // 2 // LANES), LANES), jnp.uint32),
                   jax.ShapeDtypeStruct((t, LANES), jnp.int32),
                   jax.ShapeDtypeStruct((t, LANES), F32),
                   jax.ShapeDtypeStruct((t, LANES), jnp.int32),
                   jax.ShapeDtypeStruct((1, LANES), jnp.int32)],
        scratch_shapes=[pltpu.VMEM((1, LANES), F32)],
        compiler_params=_params("arbitrary"),
        name="route",
    )(x, gain, mods, mods, rw, rb)


DMA_ISSUE_UNROLL = 8


def _dispatch_kernel(dest_ref, fill_ref, nact_ref, h_ref, xs_hbm, zero_ref, sem, zsem, *, n_blocks, s_rows):
    i = pl.program_id(0)
    tm = h_ref.shape[0] // s_rows

    def row_copy(src_ref, r, dst, s):
        return pltpu.make_async_copy(src_ref.at[pl.ds(pl.multiple_of(r * s_rows, s_rows), s_rows), :],
                                     xs_hbm.at[pl.ds(pl.multiple_of(dst * s_rows, s_rows), s_rows), :], s)

    def issue(r, carry):
        for j in range(TOP_K):
            row_copy(h_ref, r, dest_ref[(i * tm + r) * TOP_K + j], sem).start()
        return carry

    lax.fori_loop(0, tm, issue, 0, unroll=DMA_ISSUE_UNROLL // 2)

    @pl.when(i == pl.num_programs(0) - 1)
    def _():
        zero_ref[...] = jnp.zeros_like(zero_ref)
        n_experts = fill_ref.shape[0] // 2

        def fill_expert(e, carry):
            first, count = fill_ref[2 * e], fill_ref[2 * e + 1]

            def start(r, c):
                row_copy(zero_ref, 0, first + r, zsem).start()
                return c

            def wait(r, c):
                row_copy(zero_ref, 0, first + r, zsem).wait()
                return c

            lax.fori_loop(0, count, start, 0)
            lax.fori_loop(0, count, wait, 0)
            return carry

        lax.fori_loop(0, n_experts, fill_expert, 0)

        def block_copy(b):
            rows = tm * s_rows
            return pltpu.make_async_copy(zero_ref, xs_hbm.at[pl.ds(pl.multiple_of(b * rows, rows), rows), :],
                                         zsem)

        def start_block(b, c):
            block_copy(b).start()
            return c

        def wait_block(b, c):
            block_copy(b).wait()
            return c

        lax.fori_loop(nact_ref[0], n_blocks, start_block, 0)
        lax.fori_loop(nact_ref[0], n_blocks, wait_block, 0)

    for j in range(TOP_K):
        pltpu.make_async_copy(h_ref, xs_hbm.at[pl.ds(0, tm * s_rows), :], sem).wait()


def dispatch_rows(h, n_tokens, dest, fill, n_active, n_blocks, tm=ROW_TILE):
    s_rows = h.shape[0] // n_tokens
    block = (tm * s_rows, h.shape[1])
    grid_spec = pltpu.PrefetchScalarGridSpec(
        num_scalar_prefetch=3,
        grid=(n_tokens // tm,),
        in_specs=[pl.BlockSpec(block, lambda i, *_: (i, 0))],
        out_specs=pl.BlockSpec(memory_space=pl.ANY),
        scratch_shapes=[pltpu.VMEM(block, h.dtype), pltpu.SemaphoreType.DMA, pltpu.SemaphoreType.DMA],
    )
    return pl.pallas_call(
        functools.partial(_dispatch_kernel, n_blocks=n_blocks, s_rows=s_rows),
        grid_spec=grid_spec,
        out_shape=jax.ShapeDtypeStruct((n_blocks * block[0], block[1]), h.dtype),
        compiler_params=_params("arbitrary"),
        name="moe_dispatch",
    )(dest, fill, n_active, h)


def _expert_changed(i, be_ref):
    return (i == 0) | (be_ref[i] != be_ref[jnp.maximum(i - 1, 0)])


def expert_runs(counts, block_expert):
    ne = counts.shape[0]
    ids = jnp.arange(ne, dtype=jnp.int32)
    present = counts > 0
    run_of_expert = jnp.cumsum(present.astype(jnp.int32)) - 1
    later = jnp.where((ids[None, :] > ids[:, None]) & present[None, :], ids[None, :], ne)
    first_present = jnp.min(jnp.where(present, ids, ne))
    nxt = jnp.min(later, axis=1)
    next_of_expert = jnp.where(nxt == ne, first_present, nxt).astype(jnp.int32)
    n_runs = jnp.sum(present.astype(jnp.int32)).reshape(1)
    return run_of_expert[block_expert], next_of_expert[block_expert], n_runs


def _stream_weights(c, i, be_ref, run_ref, next_ref, nruns_ref, copies, casts):
    n_runs = nruns_ref[0]
    g = c * n_runs + run_ref[i]
    slot = g % 2

    @pl.when(g == 0)
    def _():
        for cp in copies(be_ref[i], c, slot):
            cp.start()

    for cp in copies(be_ref[i], c, slot):
        cp.wait()

    @pl.when(g + 1 < pl.num_programs(0) * n_runs)
    def _():
        wraps = run_ref[i] == n_runs - 1
        for cp in copies(next_ref[i], jnp.where(wraps, c + 1, c), 1 - slot):
            cp.start()

    casts(slot)


def _gmm_up_kernel(be_ref, run_ref, next_ref, nruns_ref, nact_ref, x_ref, w_hbm, bg_ref, bu_ref, o_ref,
                   wg_f32, wu_f32, wg_bf, wu_bf, sem, *, layer, tn, dff):
    c = pl.program_id(0)
    i = pl.program_id(1)

    def copies(e, cc, slot):
        col = pl.multiple_of(cc * tn, tn)
        return [pltpu.make_async_copy(w_hbm.at[layer, e, :, pl.ds(col, tn)], wg_f32.at[slot], sem.at[slot]),
                pltpu.make_async_copy(w_hbm.at[layer, e, :, pl.ds(dff + col, tn)], wu_f32.at[slot],
                                      sem.at[slot])]

    def casts(slot):
        wg_bf[...] = wg_f32[slot].astype(BF16)
        wu_bf[...] = wu_f32[slot].astype(BF16)

    @pl.when(i < nact_ref[0])
    def _():
        @pl.when(_expert_changed(i, be_ref))
        def _():
            _stream_weights(c, i, be_ref, run_ref, next_ref, nruns_ref, copies, casts)

        s_rows = x_ref.shape[0] // o_ref.shape[0]
        words = [x_ref[pl.ds(s, o_ref.shape[0], stride=s_rows), :] for s in range(s_rows)]
        x = _unpack_bf16_halves(jnp.concatenate(words, axis=1))
        g = jnp.dot(x, wg_bf[...], preferred_element_type=F32) + bg_ref[...]
        u = jnp.dot(x, wu_bf[...], preferred_element_type=F32) + bu_ref[...]
        gate = jnp.minimum(g, SWIGLU_LIMIT)
        up = jnp.clip(u, -SWIGLU_LIMIT, SWIGLU_LIMIT)
        act = (up + 1.0) * gate * _sigmoid(gate * SWIGLU_ALPHA)
        o_ref[...] = act.astype(o_ref.dtype)

    @pl.when(i >= nact_ref[0])
    def _():
        o_ref[...] = jnp.zeros_like(o_ref)


def _row_block(i, na):
    return jnp.minimum(i, na[0] - 1)


def gmm_up(xs, w_gu, b_gu, block_expert, runs, n_active, layer, tm=ROW_TILE, tn=1024):
    d = w_gu.shape[-2]
    s_rows = d // 2 // xs.shape[1]
    p = xs.shape[0] // s_rows
    dff = w_gu.shape[-1] // 2
    tn = min(tn, dff)
    nb, nc = p // tm, dff // tn
    depth, ne = b_gu.shape[:2]
    b4 = b_gu.reshape(depth, ne, 1, 2 * dff)
    grid_spec = pltpu.PrefetchScalarGridSpec(
        num_scalar_prefetch=5,
        grid=(nc, nb),
        in_specs=[
            pl.BlockSpec((tm * s_rows, xs.shape[1]),
                         lambda c, i, be, rn, nx, nr, na: (_row_block(i, na), 0)),
            pl.BlockSpec(memory_space=pl.ANY),
            pl.BlockSpec((None, None, 1, tn),
                         lambda c, i, be, rn, nx, nr, na: (layer, be[_row_block(i, na)], 0, c)),
            pl.BlockSpec((None, None, 1, tn),
                         lambda c, i, be, rn, nx, nr, na: (layer, be[_row_block(i, na)], 0, nc + c)),
        ],
        out_specs=pl.BlockSpec((tm, tn), lambda c, i, *_: (i, c)),
        scratch_shapes=[pltpu.VMEM((2, d, tn), F32), pltpu.VMEM((2, d, tn), F32),
                        pltpu.VMEM((d, tn), BF16), pltpu.VMEM((d, tn), BF16),
                        pltpu.SemaphoreType.DMA((2,))],
    )
    return pl.pallas_call(
        functools.partial(_gmm_up_kernel, layer=layer, tn=tn, dff=dff),
        grid_spec=grid_spec,
        out_shape=jax.ShapeDtypeStruct((p, dff), BF16),
        compiler_params=_params("arbitrary", "arbitrary"),
        name="moe_up",
    )(block_expert, *runs, n_active, xs, w_gu, b4, b4)


def _gmm_down_kernel(be_ref, run_ref, next_ref, nruns_ref, nact_ref, a_ref, w_hbm, b_ref, o_ref,
                     w_f32, w_bf, sem, *, layer, tn):
    c = pl.program_id(0)
    i = pl.program_id(1)

    def copies(e, cc, slot):
        col = pl.multiple_of(cc * tn, tn)
        return [pltpu.make_async_copy(w_hbm.at[layer, e, :, pl.ds(col, tn)], w_f32.at[slot], sem.at[slot])]

    def casts(slot):
        w_bf[...] = w_f32[slot].astype(BF16)

    @pl.when(i < nact_ref[0])
    def _():
        @pl.when(_expert_changed(i, be_ref))
        def _():
            _stream_weights(c, i, be_ref, run_ref, next_ref, nruns_ref, copies, casts)

        o_ref[...] = jnp.dot(a_ref[...], w_bf[...], preferred_element_type=F32) + b_ref[...]

    @pl.when(i >= nact_ref[0])
    def _():
        o_ref[...] = jnp.zeros_like(o_ref)


def gmm_down(act, w_dn, b_dn, block_expert, runs, n_active, layer, tm=ROW_TILE, tn=2048):
    p, dff = act.shape
    d = w_dn.shape[-1]
    tn = min(tn, d)
    nb, nc = p // tm, d // tn
    depth, ne = b_dn.shape[:2]
    b4 = b_dn.reshape(depth, ne, 1, d)
    grid_spec = pltpu.PrefetchScalarGridSpec(
        num_scalar_prefetch=5,
        grid=(nc, nb),
        in_specs=[
            pl.BlockSpec((tm, dff), lambda c, i, be, rn, nx, nr, na: (_row_block(i, na), 0)),
            pl.BlockSpec(memory_space=pl.ANY),
            pl.BlockSpec((None, None, 1, tn),
                         lambda c, i, be, rn, nx, nr, na: (layer, be[_row_block(i, na)], 0, c)),
        ],
        out_specs=pl.BlockSpec((tm, tn), lambda c, i, *_: (i, c)),
        scratch_shapes=[pltpu.VMEM((2, dff, tn), F32), pltpu.VMEM((dff, tn), BF16),
                        pltpu.SemaphoreType.DMA((2,))],
    )
    return pl.pallas_call(
        functools.partial(_gmm_down_kernel, layer=layer, tn=tn),
        grid_spec=grid_spec,
        out_shape=jax.ShapeDtypeStruct((p, d), F32),
        compiler_params=_params("arbitrary", "arbitrary"),
        name="moe_down",
    )(block_expert, *runs, n_active, act, w_dn, b4)


def _combine_kernel(dest_ref, y_hbm, x_ref, gate_ref, g2_ref, ng_ref, sc_ref, sh_ref, o1_ref, o2_ref,
                    buf_ref, sem, *, first_tiles):
    i = pl.program_id(0)
    tm = x_ref.shape[0]

    def issue_block(blk):
        slot = blk % 2

        def issue(r, carry):
            for j in range(TOP_K):
                src = dest_ref[(blk * tm + r) * TOP_K + j]
                pltpu.make_async_copy(y_hbm.at[pl.ds(src, 1), :],
                                      buf_ref.at[slot, j, pl.ds(r, 1), :], sem.at[slot]).start()
            return carry

        lax.fori_loop(0, tm, issue, 0, unroll=DMA_ISSUE_UNROLL // 2)

    @pl.when(i == 0)
    def _():
        issue_block(i)

    @pl.when(i + 1 < pl.num_programs(0))
    def _():
        issue_block(i + 1)

    slot = i % 2
    for j in range(TOP_K):
        pltpu.make_async_copy(y_hbm.at[pl.ds(0, tm), :], buf_ref.at[slot, j], sem.at[slot]).wait()
    gates = gate_ref[...]
    acc = gates[:, 0:1] * buf_ref[slot, 0]
    for j in range(1, TOP_K):
        acc += gates[:, j:j + 1] * buf_ref[slot, j]
    x_new = x_ref[...] + g2_ref[...] * acc
    normed = _rms(x_new) * ng_ref[...]
    if first_tiles is None:
        o1_ref[...] = x_new
        o2_ref[...] = (normed * (1.0 + sc_ref[...]) + sh_ref[...]).astype(o2_ref.dtype)
    else:
        @pl.when(i < first_tiles)
        def _():
            o1_ref[...] = normed

        @pl.when(i >= first_tiles)
        def _():
            o2_ref[...] = normed


def combine_residual(y, dest, gates, x, mods, layer, mod_of_tile, next_gain, n_first, tm=ROW_TILE):
    t, d = x.shape
    last = n_first is not None
    tile = lambda i, dst: (i, 0)
    if last:
        first_tiles = n_first // tm
        out_specs = [pl.BlockSpec((tm, d), lambda i, dst: (jnp.minimum(i, first_tiles - 1), 0)),
                     pl.BlockSpec((tm, d), lambda i, dst: (jnp.maximum(i - first_tiles, 0), 0))]
        out_shape = [jax.ShapeDtypeStruct((n_first, d), F32), jax.ShapeDtypeStruct((t - n_first, d), F32)]
        gain_spec = pl.BlockSpec((None, 1, d), lambda i, dst: (0, 0, 0))
        mod_layer = layer
    else:
        first_tiles = None
        out_specs = [pl.BlockSpec((tm, d), tile), pl.BlockSpec((tm, d), tile)]
        out_shape = [jax.ShapeDtypeStruct((t, d), F32), jax.ShapeDtypeStruct((t, d), BF16)]
        gain_spec = pl.BlockSpec((None, 1, d), lambda i, dst: (layer + 1, 0, 0))
        mod_layer = layer + 1
    grid_spec = pltpu.PrefetchScalarGridSpec(
        num_scalar_prefetch=1,
        grid=(t // tm,),
        in_specs=[
            pl.BlockSpec(memory_space=pl.ANY),
            pl.BlockSpec((tm, d), tile),
            pl.BlockSpec((tm, LANES), tile),
            _mod_spec(d, layer, 5, mod_of_tile),
            gain_spec,
            _mod_spec(d, mod_layer, 1, mod_of_tile),
            _mod_spec(d, mod_layer, 0, mod_of_tile),
        ],
        out_specs=out_specs,
        scratch_shapes=[pltpu.VMEM((2, TOP_K, tm, d), F32), pltpu.SemaphoreType.DMA((2,))],
    )
    return pl.pallas_call(
        functools.partial(_combine_kernel, first_tiles=first_tiles),
        grid_spec=grid_spec,
        out_shape=out_shape,
        compiler_params=_params("arbitrary"),
        name="moe_combine",
    )(dest, y, x, gates, mods, next_gain, mods, mods)


def moe_layer(x, gain, mods, router_w, router_b, w_gu, b_gu, w_dn, b_dn, layer, mod_of_tile,
              next_gain, n_first):
    t, d = x.shape
    ne = router_w.shape[-1]
    n_blocks = t * TOP_K // ROW_TILE + ne
    h, idx, gates, rank, counts = route(x, gain, mods, router_w, router_b, layer, mod_of_tile)
    counts = counts[0, :ne]
    padded = (counts + ROW_TILE - 1) // ROW_TILE * ROW_TILE
    ends = jnp.cumsum(padded)
    pstart = ends - padded
    e_idx = idx[:, :TOP_K]
    dest = (pstart[e_idx] + rank[:, :TOP_K]).reshape(t * TOP_K)
    block_start = jnp.arange(n_blocks, dtype=jnp.int32) * ROW_TILE
    block_expert = jnp.minimum(jnp.sum((ends[None, :] <= block_start[:, None]).astype(jnp.int32), axis=1),
                               ne - 1)
    n_active = (ends[-1:] // ROW_TILE).astype(jnp.int32)
    fill = jnp.stack([pstart + counts, padded - counts], axis=1).reshape(2 * ne).astype(jnp.int32)
    runs = expert_runs(counts, block_expert)
    xs = dispatch_rows(h, t, dest, fill, n_active, n_blocks)
    act = gmm_up(xs, w_gu, b_gu, block_expert, runs, n_active, layer)
    y = gmm_down(act, w_dn, b_dn, block_expert, runs, n_active, layer)
    return combine_residual(y, dest, gates, x, mods, layer, mod_of_tile, next_gain, n_first)


ATTN_SCALE = HEAD_DIM ** -0.5
_NT = (((1,), (1,)), ((), ()))
_TN = (((0,), (0,)), ((), ()))


def _head(h):
    return slice(h * HEAD_DIM, (h + 1) * HEAD_DIM)


def _ctx_attn_kernel(q_ref, k_ref, v_ref, o_ref):
    for h in range(N_HEADS):
        q = q_ref[:, _head(h)].astype(BF16)
        k = k_ref[:, _head(h)].astype(BF16)
        v = v_ref[:, _head(h)].astype(BF16)
        s = lax.dot_general(q, k, _NT, preferred_element_type=F32) * ATTN_SCALE
        e = jnp.exp(s - jnp.max(s, axis=-1, keepdims=True))
        p = e / jnp.sum(e, axis=-1, keepdims=True)
        o_ref[:, _head(h)] = jnp.dot(p.astype(BF16), v, preferred_element_type=F32).astype(o_ref.dtype)


def context_attention(proj, n_seq, seq_len):
    def spec(group):
        return pl.BlockSpec((seq_len, MIX_W), lambda b: (b, group))

    return pl.pallas_call(
        _ctx_attn_kernel,
        grid=(n_seq,),
        in_specs=[spec(0), spec(1), spec(2)],
        out_specs=pl.BlockSpec((seq_len, MIX_W), lambda b: (b, 0)),
        out_shape=jax.ShapeDtypeStruct((n_seq * seq_len, MIX_W), BF16),
        compiler_params=_params("arbitrary"),
        name="ctx_attn",
    )(proj, proj, proj)


def na_bias_table(rpb):
    depth, nh = rpb.shape[:2]
    qc = np.arange(GRID_W)[:, None]
    kc = np.arange(GRID_W)[None, :]
    d_col = np.clip(kc - qc + NA_WIN_COLS - 1, 0, 2 * NA_WIN_COLS - 2)
    c0 = np.clip(qc - NA_WIN_COLS // 2, 0, GRID_W - NA_WIN_COLS)
    inside = (kc >= c0) & (kc < c0 + NA_WIN_COLS)
    onehot = (d_col.reshape(-1)[None, :] == np.arange(2 * NA_WIN_COLS - 1)[:, None]).astype(np.float32)
    tt = jnp.einsum('lhij,jm->lhim', rpb, jnp.asarray(onehot), precision=lax.Precision.HIGHEST)
    tt = jnp.where(inside[None, None, None], tt.reshape(depth, nh, -1, GRID_W, GRID_W), NEG_BIG)
    return jnp.stack([jnp.concatenate([tt[:, :, i0 + kk] for kk in range(NA_WIN_ROWS)], axis=-1)
                      for i0 in range(NA_WIN_ROWS)], axis=2)


def _na_attn_kernel(q_ref, k_ref, v_ref, ck_ref, cv_ref, bias_ref, o_ref, *, rows):
    r = pl.program_id(1)
    r0 = jnp.clip(r - NA_WIN_ROWS // 2, 0, rows - NA_WIN_ROWS)
    start = pl.multiple_of(r0 * GRID_W, GRID_W)
    win = NA_WIN_ROWS * GRID_W
    for h in range(N_HEADS):
        q = q_ref[:, _head(h)].astype(BF16)
        kw = k_ref[pl.ds(start, win), _head(h)].astype(BF16)
        vw = v_ref[pl.ds(start, win), _head(h)].astype(BF16)
        ck = ck_ref[:, _head(h)].astype(BF16)
        cv = cv_ref[:, _head(h)].astype(BF16)
        s_loc = lax.dot_general(q, kw, _NT, preferred_element_type=F32) * ATTN_SCALE + bias_ref[h]
        s_ctx = lax.dot_general(q, ck, _NT, preferred_element_type=F32) * ATTN_SCALE
        m = jnp.maximum(jnp.max(s_loc, axis=-1, keepdims=True), jnp.max(s_ctx, axis=-1, keepdims=True))
        e_loc = jnp.exp(s_loc - m)
        e_ctx = jnp.exp(s_ctx - m)
        den = jnp.sum(e_loc, axis=-1, keepdims=True) + jnp.sum(e_ctx, axis=-1, keepdims=True)
        o = (jnp.dot((e_loc / den).astype(BF16), vw, preferred_element_type=F32)
             + jnp.dot((e_ctx / den).astype(BF16), cv, preferred_element_type=F32))
        o_ref[:, _head(h)] = o.astype(o_ref.dtype)


def neighbourhood_attention(proj, cache_k, cache_v, bias_table, layer, row0, n_seq, seq_len):
    rows = seq_len // GRID_W
    assert rows >= NA_WIN_ROWS and row0 % seq_len == 0
    ctx = cache_k.shape[2]
    ck, cv = cache_k, cache_v

    def offset_in_window(r):
        return jnp.clip(r - NA_WIN_ROWS // 2, 0, rows - NA_WIN_ROWS) - r + NA_WIN_ROWS - 1

    def seq_spec(group):
        return pl.BlockSpec((seq_len, MIX_W), lambda b, r: (row0 // seq_len + b, group))

    ctx_spec = pl.BlockSpec((None, None, ctx, MIX_W), lambda b, r: (b, layer, 0, 0))
    return pl.pallas_call(
        functools.partial(_na_attn_kernel, rows=rows),
        grid=(n_seq, rows),
        in_specs=[
            pl.BlockSpec((GRID_W, MIX_W), lambda b, r: (row0 // GRID_W + b * rows + r, 0)),
            seq_spec(1), seq_spec(2), ctx_spec, ctx_spec,
            pl.BlockSpec((None, N_HEADS, None, GRID_W, NA_WIN_ROWS * GRID_W),
                         lambda b, r: (layer, 0, offset_in_window(r), 0, 0)),
        ],
        out_specs=pl.BlockSpec((GRID_W, MIX_W), lambda b, r: (b * rows + r, 0)),
        out_shape=jax.ShapeDtypeStruct((n_seq * seq_len, MIX_W), BF16),
        compiler_params=_params("arbitrary", "arbitrary"),
        name="na_attn",
    )(proj, proj, proj, ck, cv, bias_table)


def _gelu_tanh(x):
    return 0.5 * x * (1.0 + jnp.tanh(math.sqrt(2.0 / math.pi) * (x + 0.044715 * (x * x * x))))


def _sgu_kernel(u_ref, v_ref, w_ref, b_ref, o_ref):
    for g in range(N_HEADS):
        u = _gelu_tanh(u_ref[:, _head(g)])
        v = _gelu_tanh(v_ref[:, _head(g)])
        vn = (v * lax.rsqrt(jnp.mean(v * v, axis=-1, keepdims=True) + EPS)).astype(BF16)
        mixed = jnp.dot(w_ref[g].astype(BF16), vn, preferred_element_type=F32) + b_ref[:, _head(g)]
        o_ref[:, _head(g)] = (u * mixed).astype(o_ref.dtype)


def spatial_gating(proj, sgu_w, sgu_b, layer):
    t = proj.shape[0]
    ch = sgu_w.shape[-1]
    depth = sgu_w.shape[0]
    bias = jnp.repeat(jnp.swapaxes(sgu_b, 1, 2), HEAD_DIM, axis=2)
    return pl.pallas_call(
        _sgu_kernel,
        grid=(t // ch,),
        in_specs=[
            pl.BlockSpec((ch, MIX_W), lambda i: (i, 3)),
            pl.BlockSpec((ch, MIX_W), lambda i: (i, 4)),
            pl.BlockSpec((None, N_HEADS, ch, ch), lambda i: (layer, 0, 0, 0)),
            pl.BlockSpec((None, ch, MIX_W), lambda i: (layer, 0, 0)),
        ],
        out_specs=pl.BlockSpec((ch, MIX_W), lambda i: (i, 0)),
        out_shape=jax.ShapeDtypeStruct((t, MIX_W), BF16),
        compiler_params=_params("arbitrary"),
        name="sgu",
    )(proj, proj, sgu_w, bias)


def _pool_kernel(x_ref, w_ref, s_ref, o_ref):
    n = x_ref.shape[0]
    t = lax.broadcasted_iota(jnp.int32, (n, n), 0)
    s = lax.broadcasted_iota(jnp.int32, (n, n), 1)
    tc = lax.broadcasted_iota(jnp.int32, (n, 1), 0)
    for g, win in enumerate(POOL_WINDOWS):
        half = win // 2
        band = jnp.where(s >= t - half, jnp.where(s < t + half, 1.0, 0.0), 0.0).astype(BF16)
        cnt = (jnp.minimum(tc + half, n) - jnp.maximum(tc - half, 0)).astype(F32)
        x = x_ref[:, _head(g)]
        hi = x.astype(BF16)
        lo = (x - hi.astype(F32)).astype(BF16)
        tot = jnp.dot(band, hi, preferred_element_type=F32) + jnp.dot(band, lo, preferred_element_type=F32)
        pooled = tot / cnt - x
        y = jnp.dot(pooled.astype(BF16), w_ref[g].astype(BF16), preferred_element_type=F32)
        o_ref[:, _head(g)] = (y * s_ref[:, _head(g)]).astype(o_ref.dtype)


def multiscale_pool(proj, pool_w, pool_scale, layer, row0, n_seq, seq_len):
    depth = pool_w.shape[0]
    return pl.pallas_call(
        _pool_kernel,
        grid=(n_seq,),
        in_specs=[
            pl.BlockSpec((seq_len, MIX_W), lambda b: (row0 // seq_len + b, N_PROJ - 1)),
            pl.BlockSpec((None, N_HEADS, HEAD_DIM, HEAD_DIM), lambda b: (layer, 0, 0, 0)),
            pl.BlockSpec((None, 1, MIX_W), lambda b: (layer, 0, 0)),
        ],
        out_specs=pl.BlockSpec((seq_len, MIX_W), lambda b: (b, 0)),
        out_shape=jax.ShapeDtypeStruct((n_seq * seq_len, MIX_W), BF16),
        compiler_params=_params("arbitrary"),
        name="pool",
    )(proj, pool_w, pool_scale.reshape(depth, 1, MIX_W))


HGRN_BLOCK = 128
HGRN_DIAG = 8


def _log_forget_k(z, lb):
    a = jnp.log(lb + LB_TINY)
    b = jnp.log1p(-lb) + (jnp.minimum(z, 0.0) - jnp.log1p(jnp.exp(-jnp.abs(z))))
    return jnp.maximum(a, b) + jnp.log1p(jnp.exp(-jnp.abs(a - b)))


def _rows_of(x, idx, span):
    return jnp.concatenate([jnp.broadcast_to(x[i:i + 1, :], (span, x.shape[1])) for i in idx], axis=0)


def _hgrn_block(q, k, v, logf, s0, rev):
    n = HGRN_BLOCK
    row = lax.broadcasted_iota(jnp.int32, (n, n), 0)
    col = lax.broadcasted_iota(jnp.int32, (n, n), 1)
    cum = logf
    sh = 1
    while sh < n:
        if rev:
            cum = cum + jnp.where(row < n - sh, pltpu.roll(cum, n - sh, 0), 0.0)
        else:
            cum = cum + jnp.where(row >= sh, pltpu.roll(cum, sh, 0), 0.0)
        sh *= 2
    att = jnp.zeros((n, n), F32)
    half = n // 2
    while half >= HGRN_DIAG:
        span = 2 * half
        pos = row % span
        first = (pos >= half) if rev else (pos < half)
        edge = half if rev else half - 1
        ref = _rows_of(cum, [j * span + edge for j in range(n // span)], span)
        qe = jnp.where(first, 0.0, q * jnp.exp(cum - ref))
        ke = jnp.where(first, k * jnp.exp(ref - cum), 0.0)
        a = lax.dot_general(qe.astype(BF16), ke.astype(BF16), _NT, preferred_element_type=F32)
        att = att + jnp.where(row // span == col // span, a, 0.0)
        half //= 2
    pos = row % HGRN_DIAG
    group_col = row - pos

    def row_of_group(x, j):
        x3 = x.reshape(n // HGRN_DIAG, HGRN_DIAG, n)
        return jnp.broadcast_to(x3[:, j:j + 1, :], x3.shape).reshape(n, n)

    for j in range(HGRN_DIAG):
        a = q * row_of_group(k, j) * jnp.exp(cum - row_of_group(cum, j))
        a = jnp.where((pos <= j) if rev else (pos >= j), a, 0.0)
        w = jnp.sum(a, axis=-1, keepdims=True)
        att = att + jnp.where(col == group_col + j, w, 0.0)
    o = jnp.dot(att.astype(BF16), v.astype(BF16), preferred_element_type=F32)
    o = o + jnp.dot((q * jnp.exp(cum)).astype(BF16), s0.astype(BF16), preferred_element_type=F32)
    last = cum[0:1, :] if rev else cum[n - 1:n, :]
    kd = (k * jnp.exp(last - cum)).astype(BF16)
    upd = lax.dot_general(kd, v.astype(BF16), _TN, preferred_element_type=F32)
    keep = jnp.sum(jnp.where(row == col, jnp.exp(last), 0.0), axis=-1, keepdims=True)
    return o, keep * s0 + upd


def _hgrn_kernel(qf_ref, if_ref, ff_ref, qb_ref, ib_ref, fb_ref, lbf_ref, lbb_ref, s0_ref,
                 of_ref, ob_ref, sfin_ref, state_ref, *, step_info):
    from_zero, c, nblk = step_info(pl.program_id(0))[:3]

    @pl.when((c == 0) & from_zero)
    def _():
        state_ref[...] = jnp.zeros_like(state_ref)

    @pl.when((c == 0) & jnp.logical_not(from_zero))
    def _():
        state_ref[...] = s0_ref[...]

    for rev, (q_ref, i_ref, f_ref, lb_ref, o_ref) in enumerate(
            ((qf_ref, if_ref, ff_ref, lbf_ref, of_ref), (qb_ref, ib_ref, fb_ref, lbb_ref, ob_ref))):
        for h in range(N_HEADS):
            q = q_ref[:, _head(h)]
            q = q * _sigmoid(q)
            logf = _log_forget_k(f_ref[:, _head(h)], lb_ref[:, _head(h)])
            k = 1.0 - jnp.exp(logf)
            o, s1 = _hgrn_block(q, k, i_ref[:, _head(h)], logf, state_ref[rev, h], bool(rev))
            o_ref[:, _head(h)] = o
            state_ref[rev, h] = s1

    @pl.when((c == nblk - 1) & from_zero)
    def _():
        sfin_ref[...] = state_ref[...]


def hgrn2_scan(proj, lb_f, lb_b, state_in, layer, n_zero, len_zero, n_init, len_init):
    blk = HGRN_BLOCK
    nz, ni = len_zero // blk, len_init // blk
    steps_zero = n_zero * nz

    def step_info(s):
        from_zero = s < steps_zero
        s2 = s - steps_zero
        b = jnp.where(from_zero, s // nz, s2 // ni)
        c = jnp.where(from_zero, s % nz, s2 % ni)
        nblk = jnp.where(from_zero, nz, ni)
        base = jnp.where(from_zero, b * nz, steps_zero + b * ni)
        return from_zero, c, nblk, b, base

    def fwd(group):
        def index(s):
            _, c, _, _, base = step_info(s)
            return (base + c, group)
        return pl.BlockSpec((blk, MIX_W), index)

    def bwd(group):
        def index(s):
            _, c, nblk, _, base = step_info(s)
            return (base + nblk - 1 - c, group)
        return pl.BlockSpec((blk, MIX_W), index)

    def init_index(s):
        from_zero, _, _, b, _ = step_info(s)
        return (jnp.where(from_zero, 0, b), layer, 0, 0, 0, 0)

    def final_index(s):
        from_zero, _, _, b, _ = step_info(s)
        return (jnp.where(from_zero, b, n_zero - 1), 0, 0, 0, 0)

    t = proj.shape[0]
    state_block = (None, 2, N_HEADS, HEAD_DIM, HEAD_DIM)
    lb_spec = pl.BlockSpec((1, MIX_W), lambda s: (0, 0))
    return pl.pallas_call(
        functools.partial(_hgrn_kernel, step_info=step_info),
        grid=(steps_zero + n_init * ni,),
        in_specs=[fwd(5), fwd(6), fwd(7), bwd(5), bwd(6), bwd(8), lb_spec, lb_spec,
                  pl.BlockSpec((None, None, 2, N_HEADS, HEAD_DIM, HEAD_DIM), init_index)],
        out_specs=[fwd(0), bwd(0), pl.BlockSpec(state_block, final_index)],
        out_shape=[jax.ShapeDtypeStruct((t, MIX_W), F32), jax.ShapeDtypeStruct((t, MIX_W), F32),
                   jax.ShapeDtypeStruct((n_zero, 2, N_HEADS, HEAD_DIM, HEAD_DIM), F32)],
        scratch_shapes=[pltpu.VMEM((2, N_HEADS, HEAD_DIM, HEAD_DIM), F32)],
        compiler_params=_params("arbitrary"),
        name="hgrn_scan",
    )(proj, proj, proj, proj, proj, proj, lb_f.reshape(1, MIX_W), lb_b.reshape(1, MIX_W), state_in)


def _hgrn_out_kernel(of_ref, ob_ref, g_ref, ng_ref, o_ref):
    for h in range(N_HEADS):
        o = of_ref[:, _head(h)] + ob_ref[:, _head(h)]
        o = o * lax.rsqrt(jnp.mean(o * o, axis=-1, keepdims=True) + EPS)
        g = g_ref[:, _head(h)]
        o_ref[:, _head(h)] = (o * ng_ref[:, _head(h)] * (g * _sigmoid(g))).astype(o_ref.dtype)


def hgrn2_output(o_f, o_b, proj, norm_g, layer, tm=512):
    t = o_f.shape[0]
    depth = norm_g.shape[0]
    tile = pl.BlockSpec((tm, MIX_W), lambda i: (i, 0))
    return pl.pallas_call(
        _hgrn_out_kernel,
        grid=(t // tm,),
        in_specs=[tile, tile, pl.BlockSpec((tm, MIX_W), lambda i: (i, 9)),
                  pl.BlockSpec((None, 1, MIX_W), lambda i: (layer, 0, 0))],
        out_specs=tile,
        out_shape=jax.ShapeDtypeStruct((t, MIX_W), BF16),
        compiler_params=_params("arbitrary"),
        name="hgrn_out",
    )(o_f, o_b, proj, norm_g.reshape(depth, 1, MIX_W))


def kernel(x_prompt, x_sample, cache_k, cache_v, state_hgrn, c, c_ctx, w_mod, b_mod, norm1_g, norm2_g, w_in, na_rpb, sgu_w, sgu_b, hgrn_lb, hgrn_norm_g, pool_w, pool_scale, w_out, router_w, router_b, exp_w_gu, exp_b_gu, exp_w_dn, exp_b_dn, final_norm_g):
    bp, sp, d = x_prompt.shape
    bs, ss, _ = x_sample.shape
    depth = w_mod.shape[0]
    tp, ts = bp * sp, bs * ss

    cond = jnp.zeros((8, d), F32).at[0].set(c_ctx).at[1:1 + bs].set(c)
    mods = adaln_all(cond, w_mod, b_mod).reshape(depth * 8 * N_MOD, 1, d)

    def mod_of_tile(tile_rows):
        prompt_tiles = tp // tile_rows
        per_seq = ss // tile_rows
        return lambda i: jnp.where(i < prompt_tiles, 0, 1 + (i - prompt_tiles) // per_seq)

    lb_p = jax.nn.softmax(hgrn_lb, axis=1)
    lb_all = jnp.maximum(jnp.cumsum(lb_p, axis=1) - lb_p[:, :1], 0.0)

    x = jnp.concatenate([x_prompt.reshape(tp, d), x_sample.reshape(ts, d)], axis=0)
    g1 = norm1_g.reshape(depth, 1, d)
    g2 = norm2_g.reshape(depth, 1, d)
    bias_table = na_bias_table(na_rpb)
    cache_k = cache_k.reshape(bs, depth, cache_k.shape[2], MIX_W)
    cache_v = cache_v.reshape(bs, depth, cache_v.shape[2], MIX_W)
    new_k, new_v, new_s = [], [], []
    h = norm_mod(x, g1, mods, 0, 1, 0, mod_of_tile(ROW_TILE))
    for l in range(depth):
        last = l == depth - 1
        proj = proj_in(h, w_in, l)
        o_a = jnp.concatenate([context_attention(proj, bp, sp),
                               neighbourhood_attention(proj, cache_k, cache_v, bias_table, l, tp, bs, ss)])
        o_b = spatial_gating(proj, sgu_w, sgu_b, l)
        o_f, o_r, s_l = hgrn2_scan(proj, lb_all[0, l], lb_all[1, l], state_hgrn, l, bp, sp, bs, ss)
        o_c = hgrn2_output(o_f, o_r, proj, hgrn_norm_g, l)
        o_d = jnp.concatenate([multiscale_pool(proj, pool_w, pool_scale, l, 0, bp, sp),
                               multiscale_pool(proj, pool_w, pool_scale, l, tp, bs, ss)])
        new_k.append(proj[:tp, MIX_W:2 * MIX_W].reshape(bp, sp, N_HEADS, HEAD_DIM))
        new_v.append(proj[:tp, 2 * MIX_W:3 * MIX_W].reshape(bp, sp, N_HEADS, HEAD_DIM))
        new_s.append(s_l)
        x = proj_out_residual([o_a, o_b, o_c, o_d], w_out, x, mods, l, mod_of_tile(1024))
        x, h = moe_layer(x, g2, mods, router_w, router_b, exp_w_gu, exp_b_gu, exp_w_dn, exp_b_dn, l,
                         mod_of_tile(ROW_TILE), final_norm_g.reshape(1, 1, d) if last else g1,
                         tp if last else None)
    return (x.reshape(bp, sp, d), h.reshape(bs, ss, d),
            jnp.stack(new_k, axis=1), jnp.stack(new_v, axis=1), jnp.stack(new_s, axis=1))
```

```python
import functools
import math

import numpy as np
import jax
import jax.numpy as jnp
from jax import lax
from jax.experimental import pallas as pl
from jax.experimental.pallas import tpu as pltpu

F32 = jnp.float32
BF16 = jnp.bfloat16

N_MOD = 6
N_PROJ = 11
MIX_W = 512
HEAD_DIM = 128
N_HEADS = 4
GRID_W = 64
NA_WIN_ROWS = 8
NA_WIN_COLS = 16
POOL_WINDOWS = (2, 4, 8, 16)
N_EXPERTS = 32
TOP_K = 4
SWIGLU_ALPHA = 1.702
SWIGLU_LIMIT = 7.0
EPS = 1e-6
LB_TINY = 1e-30
NEG_BIG = -1e30
HGRN_CHUNK = 16

LANES = 128
ROW_TILE = 256
VMEM_LIMIT = 56 * 1024 * 1024


def _params(*sem, vmem=VMEM_LIMIT):
    return pltpu.CompilerParams(dimension_semantics=sem, vmem_limit_bytes=vmem)


def _sigmoid(x):
    return 1.0 / (1.0 + jnp.exp(-x))


def _adaln_kernel(c_ref, w_ref, b_ref, o_ref):
    c = c_ref[...]
    s = (c * _sigmoid(c)).astype(BF16)
    o_ref[...] = jnp.dot(s, w_ref[...].astype(BF16), preferred_element_type=F32) + b_ref[...]


def adaln_all(cond, w_mod, b_mod, tn=1024):
    depth, d, n = w_mod.shape
    rows = cond.shape[0]
    tn = min(tn, n)
    return pl.pallas_call(
        _adaln_kernel,
        grid=(depth, n // tn),
        in_specs=[
            pl.BlockSpec((rows, d), lambda l, j: (0, 0)),
            pl.BlockSpec((None, d, tn), lambda l, j: (l, 0, j)),
            pl.BlockSpec((None, 1, tn), lambda l, j: (l, 0, j)),
        ],
        out_specs=pl.BlockSpec((None, rows, tn), lambda l, j: (l, 0, j)),
        out_shape=jax.ShapeDtypeStruct((depth, rows, n), F32),
        compiler_params=_params("arbitrary", "arbitrary"),
        name="adaln",
    )(cond, w_mod, b_mod.reshape(depth, 1, n))


def _rms(x):
    return x * lax.rsqrt(jnp.mean(x * x, axis=-1, keepdims=True) + EPS)


def _normmod_kernel(x_ref, g_ref, sc_ref, sh_ref, o_ref):
    h = (_rms(x_ref[...]) * g_ref[...]) * (1.0 + sc_ref[...]) + sh_ref[...]
    o_ref[...] = h.astype(o_ref.dtype)


def _mod_spec(d, layer, which, mod_of_tile):
    return pl.BlockSpec((None, 1, d), lambda i, *_: ((layer * 8 + mod_of_tile(i)) * N_MOD + which, 0, 0))


def norm_mod(x, gain, mods, layer, which_scale, which_shift, mod_of_tile, tm=ROW_TILE):
    t, d = x.shape
    return pl.pallas_call(
        _normmod_kernel,
        grid=(t // tm,),
        in_specs=[
            pl.BlockSpec((tm, d), lambda i: (i, 0)),
            pl.BlockSpec((None, 1, d), lambda i: (layer, 0, 0)),
            _mod_spec(d, layer, which_scale, mod_of_tile),
            _mod_spec(d, layer, which_shift, mod_of_tile),
        ],
        out_specs=pl.BlockSpec((tm, d), lambda i: (i, 0)),
        out_shape=jax.ShapeDtypeStruct((t, d), BF16),
        compiler_params=_params("arbitrary"),
        name="norm_mod",
    )(x, gain, mods, mods)


def _mm_kernel(x_ref, w_ref, o_ref, wbf_ref):
    @pl.when(pl.program_id(1) == 0)
    def _():
        wbf_ref[...] = w_ref[...].astype(BF16)

    o_ref[...] = jnp.dot(x_ref[...], wbf_ref[...], preferred_element_type=F32).astype(o_ref.dtype)


def proj_in(h, w_in, layer, tm=1024, tn=512):
    t, d = h.shape
    n = w_in.shape[-1]
    tm, tn = min(tm, t), min(tn, n)
    return pl.pallas_call(
        _mm_kernel,
        grid=(n // tn, t // tm),
        in_specs=[
            pl.BlockSpec((tm, d), lambda j, i: (i, 0)),
            pl.BlockSpec((None, d, tn), lambda j, i: (layer, 0, j)),
        ],
        out_specs=pl.BlockSpec((tm, tn), lambda j, i: (i, j)),
        out_shape=jax.ShapeDtypeStruct((t, n), F32),
        scratch_shapes=[pltpu.VMEM((d, tn), BF16)],
        compiler_params=_params("arbitrary", "arbitrary"),
        name="proj_in",
    )(h, w_in)


def _proj_out_kernel(a_ref, b_ref, c_ref, d_ref, w_ref, x_ref, g_ref, o_ref, wbf_ref):
    @pl.when(pl.program_id(1) == 0)
    def _():
        wbf_ref[...] = w_ref[...].astype(BF16)

    k = a_ref.shape[1]
    acc = jnp.dot(a_ref[...], wbf_ref[0:k, :], preferred_element_type=F32)
    acc += jnp.dot(b_ref[...], wbf_ref[k:2 * k, :], preferred_element_type=F32)
    acc += jnp.dot(c_ref[...], wbf_ref[2 * k:3 * k, :], preferred_element_type=F32)
    acc += jnp.dot(d_ref[...], wbf_ref[3 * k:4 * k, :], preferred_element_type=F32)
    o_ref[...] = x_ref[...] + g_ref[...] * acc


def proj_out_residual(parts, w_out, x, mods, layer, mod_of_tile, tm=1024, tn=512):
    t, d = x.shape
    k = parts[0].shape[1]
    tm, tn = min(tm, t), min(tn, d)
    part_spec = pl.BlockSpec((tm, k), lambda j, i: (i, 0))
    return pl.pallas_call(
        _proj_out_kernel,
        grid=(d // tn, t // tm),
        in_specs=[part_spec] * 4 + [
            pl.BlockSpec((None, d, tn), lambda j, i: (layer, 0, j)),
            pl.BlockSpec((tm, tn), lambda j, i: (i, j)),
            pl.BlockSpec((None, 1, tn),
                         lambda j, i: ((layer * 8 + mod_of_tile(i)) * N_MOD + 2, 0, j)),
        ],
        out_specs=pl.BlockSpec((tm, tn), lambda j, i: (i, j)),
        out_shape=jax.ShapeDtypeStruct((t, d), F32),
        scratch_shapes=[pltpu.VMEM((d, tn), BF16)],
        compiler_params=_params("arbitrary", "arbitrary"),
        name="proj_out",
    )(*parts, w_out, x, mods)


def _pack_bf16_halves(x):
    n = x.shape[1] // 2
    lo = lax.bitcast_convert_type(x[:, :n].astype(BF16).astype(F32), jnp.uint32)
    hi = lax.bitcast_convert_type(x[:, n:].astype(BF16).astype(F32), jnp.uint32)
    return (lo >> 16) | (hi & jnp.uint32(0xFFFF0000))


def _unpack_bf16_halves(w):
    lo = lax.bitcast_convert_type(w << 16, F32).astype(BF16)
    hi = lax.bitcast_convert_type(w & jnp.uint32(0xFFFF0000), F32).astype(BF16)
    return jnp.concatenate([lo, hi], axis=1)


def _router_kernel(x_ref, g_ref, sc_ref, sh_ref, rw_ref, rb_ref,
                   h_ref, idx_ref, gate_ref, rank_ref, cnt_ref, carry_ref):
    i = pl.program_id(0)

    @pl.when(i == 0)
    def _():
        carry_ref[...] = jnp.zeros_like(carry_ref)

    h = (_rms(x_ref[...]) * g_ref[...]) * (1.0 + sc_ref[...]) + sh_ref[...]
    packed = _pack_bf16_halves(h)
    s_rows = packed.shape[1] // LANES
    for s in range(s_rows):
        h_ref[pl.ds(s, h.shape[0], stride=s_rows), :] = packed[:, s * LANES:(s + 1) * LANES]
    h_hi = h.astype(BF16)
    h_lo = (h - h_hi.astype(F32)).astype(BF16)
    logits = (jnp.dot(h_hi, rw_ref[0], preferred_element_type=F32)
              + jnp.dot(h_lo, rw_ref[0], preferred_element_type=F32)
              + jnp.dot(h_hi, rw_ref[1], preferred_element_type=F32)) + rb_ref[...]
    tm = logits.shape[0]
    lane = lax.broadcasted_iota(jnp.int32, (tm, LANES), 1)
    vals, hots = [], []
    idx_out = jnp.zeros((tm, LANES), jnp.int32)
    for j in range(TOP_K):
        m = jnp.max(logits, axis=-1, keepdims=True)
        idx = jnp.min(jnp.where(logits == m, lane, LANES), axis=-1, keepdims=True)
        hot = lane == idx
        vals.append(m)
        hots.append(hot)
        idx_out = jnp.where(lane == j, idx, idx_out)
        logits = jnp.where(hot, -jnp.inf, logits)
    exps = [jnp.exp(v - vals[0]) for v in vals]
    denom = exps[0] + exps[1] + exps[2] + exps[3]
    gate_out = jnp.zeros((tm, LANES), F32)
    for j in range(TOP_K):
        gate_out = jnp.where(lane == j, exps[j] / denom, gate_out)
    chosen = (hots[0] | hots[1] | hots[2] | hots[3])
    chosen_f = jnp.where(chosen, 1.0, 0.0)
    row = lax.broadcasted_iota(jnp.int32, (tm, tm), 0)
    col = lax.broadcasted_iota(jnp.int32, (tm, tm), 1)
    before = jnp.where(col < row, 1.0, 0.0).astype(BF16)
    base = carry_ref[...] + jnp.dot(before, chosen_f.astype(BF16), preferred_element_type=F32)
    rank_out = jnp.zeros((tm, LANES), F32)
    for j in range(TOP_K):
        r = jnp.sum(jnp.where(hots[j], base, 0.0), axis=-1, keepdims=True)
        rank_out = jnp.where(lane == j, r, rank_out)
    carry_ref[...] += jnp.sum(chosen_f, axis=0, keepdims=True)
    idx_ref[...] = idx_out
    gate_ref[...] = gate_out
    rank_ref[...] = rank_out.astype(jnp.int32)
    cnt_ref[...] = carry_ref[...].astype(jnp.int32)


def route(x, gain, mods, router_w, router_b, layer, mod_of_tile, tm=ROW_TILE):
    t, d = x.shape
    ne = router_w.shape[-1]
    rw = jnp.pad(router_w[layer], ((0, 0), (0, LANES - ne)))
    rw_hi = rw.astype(BF16)
    rw = jnp.stack([rw_hi, (rw - rw_hi.astype(F32)).astype(BF16)])
    rb = jnp.pad(router_b[layer], (0, LANES - ne), constant_values=-jnp.inf).reshape(1, LANES)
    tile = pl.BlockSpec((tm, LANES), lambda i: (i, 0))
    return pl.pallas_call(
        _router_kernel,
        grid=(t // tm,),
        in_specs=[
            pl.BlockSpec((tm, d), lambda i: (i, 0)),
            pl.BlockSpec((None, 1, d), lambda i: (layer, 0, 0)),
            _mod_spec(d, layer, 4, mod_of_tile),
            _mod_spec(d, layer, 3, mod_of_tile),
            pl.BlockSpec((2, d, LANES), lambda i: (0, 0, 0)),
            pl.BlockSpec((1, LANES), lambda i: (0, 0)),
        ],
        out_specs=[pl.BlockSpec((tm * (d // 2 // LANES), LANES), lambda i: (i, 0)), tile, tile, tile,
                   pl.BlockSpec((1, LANES), lambda i: (0, 0))],
        out_shape=[jax.ShapeDtypeStruct((t * (d // 2 // LANES), LANES), jnp.uint32),
                   jax.ShapeDtypeStruct((t, LANES), jnp.int32),
                   jax.ShapeDtypeStruct((t, LANES), F32),
                   jax.ShapeDtypeStruct((t, LANES), jnp.int32),
                   jax.ShapeDtypeStruct((1, LANES), jnp.int32)],
        scratch_shapes=[pltpu.VMEM((1, LANES), F32)],
        compiler_params=_params("arbitrary"),
        name="route",
    )(x, gain, mods, mods, rw, rb)


DMA_ISSUE_UNROLL = 8


def _dispatch_kernel(dest_ref, fill_ref, nact_ref, h_ref, xs_hbm, zero_ref, sem, zsem, *, n_blocks, s_rows):
    i = pl.program_id(0)
    tm = h_ref.shape[0] // s_rows

    def row_copy(src_ref, r, dst, s):
        return pltpu.make_async_copy(src_ref.at[pl.ds(pl.multiple_of(r * s_rows, s_rows), s_rows), :],
                                     xs_hbm.at[pl.ds(pl.multiple_of(dst * s_rows, s_rows), s_rows), :], s)

    def issue(r, carry):
        for j in range(TOP_K):
            row_copy(h_ref, r, dest_ref[(i * tm + r) * TOP_K + j], sem).start(priority=j % 2)
        return carry

    lax.fori_loop(0, tm, issue, 0, unroll=DMA_ISSUE_UNROLL // 2)

    @pl.when(i == pl.num_programs(0) - 1)
    def _():
        zero_ref[...] = jnp.zeros_like(zero_ref)
        n_experts = fill_ref.shape[0] // 2

        def fill_expert(e, carry):
            first, count = fill_ref[2 * e], fill_ref[2 * e + 1]

            def start(r, c):
                row_copy(zero_ref, 0, first + r, zsem).start()
                return c

            def wait(r, c):
                row_copy(zero_ref, 0, first + r, zsem).wait()
                return c

            lax.fori_loop(0, count, start, 0)
            lax.fori_loop(0, count, wait, 0)
            return carry

        lax.fori_loop(0, n_experts, fill_expert, 0)

        def block_copy(b):
            rows = tm * s_rows
            return pltpu.make_async_copy(zero_ref, xs_hbm.at[pl.ds(pl.multiple_of(b * rows, rows), rows), :],
                                         zsem)

        def start_block(b, c):
            block_copy(b).start()
            return c

        def wait_block(b, c):
            block_copy(b).wait()
            return c

        lax.fori_loop(nact_ref[0], n_blocks, start_block, 0)
        lax.fori_loop(nact_ref[0], n_blocks, wait_block, 0)

    for j in range(TOP_K):
        pltpu.make_async_copy(h_ref, xs_hbm.at[pl.ds(0, tm * s_rows), :], sem).wait()


def dispatch_rows(h, n_tokens, dest, fill, n_active, n_blocks, tm=ROW_TILE):
    s_rows = h.shape[0] // n_tokens
    block = (tm * s_rows, h.shape[1])
    grid_spec = pltpu.PrefetchScalarGridSpec(
        num_scalar_prefetch=3,
        grid=(n_tokens // tm,),
        in_specs=[pl.BlockSpec(block, lambda i, *_: (i, 0))],
        out_specs=pl.BlockSpec(memory_space=pl.ANY),
        scratch_shapes=[pltpu.VMEM(block, h.dtype), pltpu.SemaphoreType.DMA, pltpu.SemaphoreType.DMA],
    )
    return pl.pallas_call(
        functools.partial(_dispatch_kernel, n_blocks=n_blocks, s_rows=s_rows),
        grid_spec=grid_spec,
        out_shape=jax.ShapeDtypeStruct((n_blocks * block[0], block[1]), h.dtype),
        compiler_params=_params("arbitrary"),
        name="moe_dispatch",
    )(dest, fill, n_active, h)


def _expert_changed(i, be_ref):
    return (i == 0) | (be_ref[i] != be_ref[jnp.maximum(i - 1, 0)])


def expert_runs(counts, block_expert):
    ne = counts.shape[0]
    ids = jnp.arange(ne, dtype=jnp.int32)
    present = counts > 0
    run_of_expert = jnp.cumsum(present.astype(jnp.int32)) - 1
    later = jnp.where((ids[None, :] > ids[:, None]) & present[None, :], ids[None, :], ne)
    first_present = jnp.min(jnp.where(present, ids, ne))
    nxt = jnp.min(later, axis=1)
    next_of_expert = jnp.where(nxt == ne, first_present, nxt).astype(jnp.int32)
    n_runs = jnp.sum(present.astype(jnp.int32)).reshape(1)
    return run_of_expert[block_expert], next_of_expert[block_expert], n_runs


def _stream_weights(c, i, be_ref, run_ref, next_ref, nruns_ref, copies, casts):
    n_runs = nruns_ref[0]
    g = c * n_runs + run_ref[i]
    slot = g % 2
    weight_queue = 1

    @pl.when(g == 0)
    def _():
        for cp in copies(be_ref[i], c, slot):
            cp.start(priority=weight_queue)

    for cp in copies(be_ref[i], c, slot):
        cp.wait()

    @pl.when(g + 1 < pl.num_programs(0) * n_runs)
    def _():
        wraps = run_ref[i] == n_runs - 1
        for cp in copies(next_ref[i], jnp.where(wraps, c + 1, c), 1 - slot):
            cp.start(priority=weight_queue)

    casts(slot)


def _gmm_up_kernel(be_ref, run_ref, next_ref, nruns_ref, nact_ref, x_ref, w_hbm, bg_ref, bu_ref, o_ref,
                   wg_f32, wu_f32, wg_bf, wu_bf, sem, *, layer, tn, dff):
    c = pl.program_id(0)
    i = pl.program_id(1)

    def copies(e, cc, slot):
        col = pl.multiple_of(cc * tn, tn)
        return [pltpu.make_async_copy(w_hbm.at[layer, e, :, pl.ds(col, tn)], wg_f32.at[slot], sem.at[slot]),
                pltpu.make_async_copy(w_hbm.at[layer, e, :, pl.ds(dff + col, tn)], wu_f32.at[slot],
                                      sem.at[slot])]

    def casts(slot):
        wg_bf[...] = wg_f32[slot].astype(BF16)
        wu_bf[...] = wu_f32[slot].astype(BF16)

    @pl.when(i < nact_ref[0])
    def _():
        @pl.when(_expert_changed(i, be_ref))
        def _():
            _stream_weights(c, i, be_ref, run_ref, next_ref, nruns_ref, copies, casts)

        s_rows = x_ref.shape[0] // o_ref.shape[0]
        words = [x_ref[pl.ds(s, o_ref.shape[0], stride=s_rows), :] for s in range(s_rows)]
        x = _unpack_bf16_halves(jnp.concatenate(words, axis=1))
        g = jnp.dot(x, wg_bf[...], preferred_element_type=F32) + bg_ref[...]
        u = jnp.dot(x, wu_bf[...], preferred_element_type=F32) + bu_ref[...]
        gate = jnp.minimum(g, SWIGLU_LIMIT)
        up = jnp.clip(u, -SWIGLU_LIMIT, SWIGLU_LIMIT)
        act = (up + 1.0) * gate * _sigmoid(gate * SWIGLU_ALPHA)
        o_ref[...] = act.astype(o_ref.dtype)

    @pl.when(i >= nact_ref[0])
    def _():
        o_ref[...] = jnp.zeros_like(o_ref)


def _row_block(i, na):
    return jnp.minimum(i, na[0] - 1)


def gmm_up(xs, w_gu, b_gu, block_expert, runs, n_active, layer, tm=ROW_TILE, tn=1024):
    d = w_gu.shape[-2]
    s_rows = d // 2 // xs.shape[1]
    p = xs.shape[0] // s_rows
    dff = w_gu.shape[-1] // 2
    tn = min(tn, dff)
    nb, nc = p // tm, dff // tn
    depth, ne = b_gu.shape[:2]
    b4 = b_gu.reshape(depth, ne, 1, 2 * dff)
    grid_spec = pltpu.PrefetchScalarGridSpec(
        num_scalar_prefetch=5,
        grid=(nc, nb),
        in_specs=[
            pl.BlockSpec((tm * s_rows, xs.shape[1]),
                         lambda c, i, be, rn, nx, nr, na: (_row_block(i, na), 0)),
            pl.BlockSpec(memory_space=pl.ANY),
            pl.BlockSpec((None, None, 1, tn),
                         lambda c, i, be, rn, nx, nr, na: (layer, be[_row_block(i, na)], 0, c)),
            pl.BlockSpec((None, None, 1, tn),
                         lambda c, i, be, rn, nx, nr, na: (layer, be[_row_block(i, na)], 0, nc + c)),
        ],
        out_specs=pl.BlockSpec((tm, tn), lambda c, i, *_: (i, c)),
        scratch_shapes=[pltpu.VMEM((2, d, tn), F32), pltpu.VMEM((2, d, tn), F32),
                        pltpu.VMEM((d, tn), BF16), pltpu.VMEM((d, tn), BF16),
                        pltpu.SemaphoreType.DMA((2,))],
    )
    return pl.pallas_call(
        functools.partial(_gmm_up_kernel, layer=layer, tn=tn, dff=dff),
        grid_spec=grid_spec,
        out_shape=jax.ShapeDtypeStruct((p, dff), BF16),
        compiler_params=_params("arbitrary", "arbitrary"),
        name="moe_up",
    )(block_expert, *runs, n_active, xs, w_gu, b4, b4)


def _gmm_down_kernel(be_ref, run_ref, next_ref, nruns_ref, nact_ref, a_ref, w_hbm, b_ref, o_ref,
                     w_f32, w_bf, sem, *, layer, tn):
    c = pl.program_id(0)
    i = pl.program_id(1)

    def copies(e, cc, slot):
        col = pl.multiple_of(cc * tn, tn)
        return [pltpu.make_async_copy(w_hbm.at[layer, e, :, pl.ds(col, tn)], w_f32.at[slot], sem.at[slot])]

    def casts(slot):
        w_bf[...] = w_f32[slot].astype(BF16)

    @pl.when(i < nact_ref[0])
    def _():
        @pl.when(_expert_changed(i, be_ref))
        def _():
            _stream_weights(c, i, be_ref, run_ref, next_ref, nruns_ref, copies, casts)

        o_ref[...] = jnp.dot(a_ref[...], w_bf[...], preferred_element_type=F32) + b_ref[...]

    @pl.when(i >= nact_ref[0])
    def _():
        o_ref[...] = jnp.zeros_like(o_ref)


def gmm_down(act, w_dn, b_dn, block_expert, runs, n_active, layer, tm=ROW_TILE, tn=2048):
    p, dff = act.shape
    d = w_dn.shape[-1]
    tn = min(tn, d)
    nb, nc = p // tm, d // tn
    depth, ne = b_dn.shape[:2]
    b4 = b_dn.reshape(depth, ne, 1, d)
    grid_spec = pltpu.PrefetchScalarGridSpec(
        num_scalar_prefetch=5,
        grid=(nc, nb),
        in_specs=[
            pl.BlockSpec((tm, dff), lambda c, i, be, rn, nx, nr, na: (_row_block(i, na), 0)),
            pl.BlockSpec(memory_space=pl.ANY),
            pl.BlockSpec((None, None, 1, tn),
                         lambda c, i, be, rn, nx, nr, na: (layer, be[_row_block(i, na)], 0, c)),
        ],
        out_specs=pl.BlockSpec((tm, tn), lambda c, i, *_: (i, c)),
        scratch_shapes=[pltpu.VMEM((2, dff, tn), F32), pltpu.VMEM((dff, tn), BF16),
                        pltpu.SemaphoreType.DMA((2,))],
    )
    return pl.pallas_call(
        functools.partial(_gmm_down_kernel, layer=layer, tn=tn),
        grid_spec=grid_spec,
        out_shape=jax.ShapeDtypeStruct((p, d), F32),
        compiler_params=_params("arbitrary", "arbitrary"),
        name="moe_down",
    )(block_expert, *runs, n_active, act, w_dn, b4)


def _combine_kernel(dest_ref, y_hbm, x_ref, gate_ref, g2_ref, ng_ref, sc_ref, sh_ref, o1_ref, o2_ref,
                    buf_ref, sem, *, first_tiles):
    i = pl.program_id(0)
    tm = x_ref.shape[0]

    def issue_block(blk):
        slot = blk % 2

        def issue(r, carry):
            for j in range(TOP_K):
                src = dest_ref[(blk * tm + r) * TOP_K + j]
                pltpu.make_async_copy(y_hbm.at[pl.ds(src, 1), :],
                                      buf_ref.at[slot, j, pl.ds(r, 1), :],
                                      sem.at[slot]).start(priority=j % 2)
            return carry

        lax.fori_loop(0, tm, issue, 0, unroll=DMA_ISSUE_UNROLL // 2)

    @pl.when(i == 0)
    def _():
        issue_block(i)

    @pl.when(i + 1 < pl.num_programs(0))
    def _():
        issue_block(i + 1)

    slot = i % 2
    for j in range(TOP_K):
        pltpu.make_async_copy(y_hbm.at[pl.ds(0, tm), :], buf_ref.at[slot, j], sem.at[slot]).wait()
    gates = gate_ref[...]
    acc = gates[:, 0:1] * buf_ref[slot, 0]
    for j in range(1, TOP_K):
        acc += gates[:, j:j + 1] * buf_ref[slot, j]
    x_new = x_ref[...] + g2_ref[...] * acc
    normed = _rms(x_new) * ng_ref[...]
    if first_tiles is None:
        o1_ref[...] = x_new
        o2_ref[...] = (normed * (1.0 + sc_ref[...]) + sh_ref[...]).astype(o2_ref.dtype)
    else:
        @pl.when(i < first_tiles)
        def _():
            o1_ref[...] = normed

        @pl.when(i >= first_tiles)
        def _():
            o2_ref[...] = normed


def combine_residual(y, dest, gates, x, mods, layer, mod_of_tile, next_gain, n_first, tm=ROW_TILE):
    t, d = x.shape
    last = n_first is not None
    tile = lambda i, dst: (i, 0)
    if last:
        first_tiles = n_first // tm
        out_specs = [pl.BlockSpec((tm, d), lambda i, dst: (jnp.minimum(i, first_tiles - 1), 0)),
                     pl.BlockSpec((tm, d), lambda i, dst: (jnp.maximum(i - first_tiles, 0), 0))]
        out_shape = [jax.ShapeDtypeStruct((n_first, d), F32), jax.ShapeDtypeStruct((t - n_first, d), F32)]
        gain_spec = pl.BlockSpec((None, 1, d), lambda i, dst: (0, 0, 0))
        mod_layer = layer
    else:
        first_tiles = None
        out_specs = [pl.BlockSpec((tm, d), tile), pl.BlockSpec((tm, d), tile)]
        out_shape = [jax.ShapeDtypeStruct((t, d), F32), jax.ShapeDtypeStruct((t, d), BF16)]
        gain_spec = pl.BlockSpec((None, 1, d), lambda i, dst: (layer + 1, 0, 0))
        mod_layer = layer + 1
    grid_spec = pltpu.PrefetchScalarGridSpec(
        num_scalar_prefetch=1,
        grid=(t // tm,),
        in_specs=[
            pl.BlockSpec(memory_space=pl.ANY),
            pl.BlockSpec((tm, d), tile),
            pl.BlockSpec((tm, LANES), tile),
            _mod_spec(d, layer, 5, mod_of_tile),
            gain_spec,
            _mod_spec(d, mod_layer, 1, mod_of_tile),
            _mod_spec(d, mod_layer, 0, mod_of_tile),
        ],
        out_specs=out_specs,
        scratch_shapes=[pltpu.VMEM((2, TOP_K, tm, d), F32), pltpu.SemaphoreType.DMA((2,))],
    )
    return pl.pallas_call(
        functools.partial(_combine_kernel, first_tiles=first_tiles),
        grid_spec=grid_spec,
        out_shape=out_shape,
        compiler_params=_params("arbitrary"),
        name="moe_combine",
    )(dest, y, x, gates, mods, next_gain, mods, mods)


def moe_layer(x, gain, mods, router_w, router_b, w_gu, b_gu, w_dn, b_dn, layer, mod_of_tile,
              next_gain, n_first):
    t, d = x.shape
    ne = router_w.shape[-1]
    n_blocks = t * TOP_K // ROW_TILE + ne
    h, idx, gates, rank, counts = route(x, gain, mods, router_w, router_b, layer, mod_of_tile)
    counts = counts[0, :ne]
    padded = (counts + ROW_TILE - 1) // ROW_TILE * ROW_TILE
    ends = jnp.cumsum(padded)
    pstart = ends - padded
    e_idx = idx[:, :TOP_K]
    dest = (pstart[e_idx] + rank[:, :TOP_K]).reshape(t * TOP_K)
    block_start = jnp.arange(n_blocks, dtype=jnp.int32) * ROW_TILE
    block_expert = jnp.minimum(jnp.sum((ends[None, :] <= block_start[:, None]).astype(jnp.int32), axis=1),
                               ne - 1)
    n_active = (ends[-1:] // ROW_TILE).astype(jnp.int32)
    fill = jnp.stack([pstart + counts, padded - counts], axis=1).reshape(2 * ne).astype(jnp.int32)
    runs = expert_runs(counts, block_expert)
    xs = dispatch_rows(h, t, dest, fill, n_active, n_blocks)
    act = gmm_up(xs, w_gu, b_gu, block_expert, runs, n_active, layer)
    y = gmm_down(act, w_dn, b_dn, block_expert, runs, n_active, layer)
    return combine_residual(y, dest, gates, x, mods, layer, mod_of_tile, next_gain, n_first)


ATTN_SCALE = HEAD_DIM ** -0.5
_NT = (((1,), (1,)), ((), ()))
_TN = (((0,), (0,)), ((), ()))


def _head(h):
    return slice(h * HEAD_DIM, (h + 1) * HEAD_DIM)


def _ctx_attn_kernel(q_ref, k_ref, v_ref, o_ref):
    for h in range(N_HEADS):
        q = q_ref[:, _head(h)].astype(BF16)
        k = k_ref[:, _head(h)].astype(BF16)
        v = v_ref[:, _head(h)].astype(BF16)
        s = lax.dot_general(q, k, _NT, preferred_element_type=F32) * ATTN_SCALE
        e = jnp.exp(s - jnp.max(s, axis=-1, keepdims=True))
        p = e / jnp.sum(e, axis=-1, keepdims=True)
        o_ref[:, _head(h)] = jnp.dot(p.astype(BF16), v, preferred_element_type=F32).astype(o_ref.dtype)


def context_attention(proj, n_seq, seq_len):
    def spec(group):
        return pl.BlockSpec((seq_len, MIX_W), lambda b: (b, group))

    return pl.pallas_call(
        _ctx_attn_kernel,
        grid=(n_seq,),
        in_specs=[spec(0), spec(1), spec(2)],
        out_specs=pl.BlockSpec((seq_len, MIX_W), lambda b: (b, 0)),
        out_shape=jax.ShapeDtypeStruct((n_seq * seq_len, MIX_W), BF16),
        compiler_params=_params("arbitrary"),
        name="ctx_attn",
    )(proj, proj, proj)


def na_bias_table(rpb):
    depth, nh = rpb.shape[:2]
    qc = np.arange(GRID_W)[:, None]
    kc = np.arange(GRID_W)[None, :]
    d_col = np.clip(kc - qc + NA_WIN_COLS - 1, 0, 2 * NA_WIN_COLS - 2)
    c0 = np.clip(qc - NA_WIN_COLS // 2, 0, GRID_W - NA_WIN_COLS)
    inside = (kc >= c0) & (kc < c0 + NA_WIN_COLS)
    onehot = (d_col.reshape(-1)[None, :] == np.arange(2 * NA_WIN_COLS - 1)[:, None]).astype(np.float32)
    tt = jnp.einsum('lhij,jm->lhim', rpb, jnp.asarray(onehot), precision=lax.Precision.HIGHEST)
    tt = jnp.where(inside[None, None, None], tt.reshape(depth, nh, -1, GRID_W, GRID_W), NEG_BIG)
    return jnp.stack([jnp.concatenate([tt[:, :, i0 + kk] for kk in range(NA_WIN_ROWS)], axis=-1)
                      for i0 in range(NA_WIN_ROWS)], axis=2)


def _na_attn_kernel(q_ref, k_ref, v_ref, ck_ref, cv_ref, bias_ref, o_ref, *, rows):
    r = pl.program_id(1)
    r0 = jnp.clip(r - NA_WIN_ROWS // 2, 0, rows - NA_WIN_ROWS)
    start = pl.multiple_of(r0 * GRID_W, GRID_W)
    win = NA_WIN_ROWS * GRID_W
    for h in range(N_HEADS):
        q = q_ref[:, _head(h)].astype(BF16)
        kw = k_ref[pl.ds(start, win), _head(h)].astype(BF16)
        vw = v_ref[pl.ds(start, win), _head(h)].astype(BF16)
        ck = ck_ref[:, _head(h)].astype(BF16)
        cv = cv_ref[:, _head(h)].astype(BF16)
        s_loc = lax.dot_general(q, kw, _NT, preferred_element_type=F32) * ATTN_SCALE + bias_ref[h]
        s_ctx = lax.dot_general(q, ck, _NT, preferred_element_type=F32) * ATTN_SCALE
        m = jnp.maximum(jnp.max(s_loc, axis=-1, keepdims=True), jnp.max(s_ctx, axis=-1, keepdims=True))
        e_loc = jnp.exp(s_loc - m)
        e_ctx = jnp.exp(s_ctx - m)
        den = jnp.sum(e_loc, axis=-1, keepdims=True) + jnp.sum(e_ctx, axis=-1, keepdims=True)
        o = (jnp.dot((e_loc / den).astype(BF16), vw, preferred_element_type=F32)
             + jnp.dot((e_ctx / den).astype(BF16), cv, preferred_element_type=F32))
        o_ref[:, _head(h)] = o.astype(o_ref.dtype)


def neighbourhood_attention(proj, cache_k, cache_v, bias_table, layer, row0, n_seq, seq_len):
    rows = seq_len // GRID_W
    assert rows >= NA_WIN_ROWS and row0 % seq_len == 0
    ctx = cache_k.shape[2]
    ck, cv = cache_k, cache_v

    def offset_in_window(r):
        return jnp.clip(r - NA_WIN_ROWS // 2, 0, rows - NA_WIN_ROWS) - r + NA_WIN_ROWS - 1

    def seq_spec(group):
        return pl.BlockSpec((seq_len, MIX_W), lambda b, r: (row0 // seq_len + b, group))

    ctx_spec = pl.BlockSpec((None, None, ctx, MIX_W), lambda b, r: (b, layer, 0, 0))
    return pl.pallas_call(
        functools.partial(_na_attn_kernel, rows=rows),
        grid=(n_seq, rows),
        in_specs=[
            pl.BlockSpec((GRID_W, MIX_W), lambda b, r: (row0 // GRID_W + b * rows + r, 0)),
            seq_spec(1), seq_spec(2), ctx_spec, ctx_spec,
            pl.BlockSpec((None, N_HEADS, None, GRID_W, NA_WIN_ROWS * GRID_W),
                         lambda b, r: (layer, 0, offset_in_window(r), 0, 0)),
        ],
        out_specs=pl.BlockSpec((GRID_W, MIX_W), lambda b, r: (b * rows + r, 0)),
        out_shape=jax.ShapeDtypeStruct((n_seq * seq_len, MIX_W), BF16),
        compiler_params=_params("arbitrary", "arbitrary"),
        name="na_attn",
    )(proj, proj, proj, ck, cv, bias_table)


def _gelu_tanh(x):
    return 0.5 * x * (1.0 + jnp.tanh(math.sqrt(2.0 / math.pi) * (x + 0.044715 * (x * x * x))))


def _sgu_kernel(u_ref, v_ref, w_ref, b_ref, o_ref):
    for g in range(N_HEADS):
        u = _gelu_tanh(u_ref[:, _head(g)])
        v = _gelu_tanh(v_ref[:, _head(g)])
        vn = (v * lax.rsqrt(jnp.mean(v * v, axis=-1, keepdims=True) + EPS)).astype(BF16)
        mixed = jnp.dot(w_ref[g].astype(BF16), vn, preferred_element_type=F32) + b_ref[:, _head(g)]
        o_ref[:, _head(g)] = (u * mixed).astype(o_ref.dtype)


def spatial_gating(proj, sgu_w, sgu_b, layer):
    t = proj.shape[0]
    ch = sgu_w.shape[-1]
    depth = sgu_w.shape[0]
    bias = jnp.repeat(jnp.swapaxes(sgu_b, 1, 2), HEAD_DIM, axis=2)
    return pl.pallas_call(
        _sgu_kernel,
        grid=(t // ch,),
        in_specs=[
            pl.BlockSpec((ch, MIX_W), lambda i: (i, 3)),
            pl.BlockSpec((ch, MIX_W), lambda i: (i, 4)),
            pl.BlockSpec((None, N_HEADS, ch, ch), lambda i: (layer, 0, 0, 0)),
            pl.BlockSpec((None, ch, MIX_W), lambda i: (layer, 0, 0)),
        ],
        out_specs=pl.BlockSpec((ch, MIX_W), lambda i: (i, 0)),
        out_shape=jax.ShapeDtypeStruct((t, MIX_W), BF16),
        compiler_params=_params("arbitrary"),
        name="sgu",
    )(proj, proj, sgu_w, bias)


def _pool_kernel(x_ref, w_ref, s_ref, o_ref):
    n = x_ref.shape[0]
    t = lax.broadcasted_iota(jnp.int32, (n, n), 0)
    s = lax.broadcasted_iota(jnp.int32, (n, n), 1)
    tc = lax.broadcasted_iota(jnp.int32, (n, 1), 0)
    for g, win in enumerate(POOL_WINDOWS):
        half = win // 2
        band = jnp.where(s >= t - half, jnp.where(s < t + half, 1.0, 0.0), 0.0).astype(BF16)
        cnt = (jnp.minimum(tc + half, n) - jnp.maximum(tc - half, 0)).astype(F32)
        x = x_ref[:, _head(g)]
        hi = x.astype(BF16)
        lo = (x - hi.astype(F32)).astype(BF16)
        tot = jnp.dot(band, hi, preferred_element_type=F32) + jnp.dot(band, lo, preferred_element_type=F32)
        pooled = tot / cnt - x
        y = jnp.dot(pooled.astype(BF16), w_ref[g].astype(BF16), preferred_element_type=F32)
        o_ref[:, _head(g)] = (y * s_ref[:, _head(g)]).astype(o_ref.dtype)


def multiscale_pool(proj, pool_w, pool_scale, layer, row0, n_seq, seq_len):
    depth = pool_w.shape[0]
    return pl.pallas_call(
        _pool_kernel,
        grid=(n_seq,),
        in_specs=[
            pl.BlockSpec((seq_len, MIX_W), lambda b: (row0 // seq_len + b, N_PROJ - 1)),
            pl.BlockSpec((None, N_HEADS, HEAD_DIM, HEAD_DIM), lambda b: (layer, 0, 0, 0)),
            pl.BlockSpec((None, 1, MIX_W), lambda b: (layer, 0, 0)),
        ],
        out_specs=pl.BlockSpec((seq_len, MIX_W), lambda b: (b, 0)),
        out_shape=jax.ShapeDtypeStruct((n_seq * seq_len, MIX_W), BF16),
        compiler_params=_params("arbitrary"),
        name="pool",
    )(proj, pool_w, pool_scale.reshape(depth, 1, MIX_W))


HGRN_BLOCK = 128
HGRN_DIAG = 8


def _log_forget_k(z, lb):
    a = jnp.log(lb + LB_TINY)
    b = jnp.log1p(-lb) + (jnp.minimum(z, 0.0) - jnp.log1p(jnp.exp(-jnp.abs(z))))
    return jnp.maximum(a, b) + jnp.log1p(jnp.exp(-jnp.abs(a - b)))


def _rows_of(x, idx, span):
    return jnp.concatenate([jnp.broadcast_to(x[i:i + 1, :], (span, x.shape[1])) for i in idx], axis=0)


def _hgrn_block(q, k, v, logf, s0, rev):
    n = HGRN_BLOCK
    row = lax.broadcasted_iota(jnp.int32, (n, n), 0)
    col = lax.broadcasted_iota(jnp.int32, (n, n), 1)
    cum = logf
    sh = 1
    while sh < n:
        if rev:
            cum = cum + jnp.where(row < n - sh, pltpu.roll(cum, n - sh, 0), 0.0)
        else:
            cum = cum + jnp.where(row >= sh, pltpu.roll(cum, sh, 0), 0.0)
        sh *= 2
    att = jnp.zeros((n, n), F32)
    half = n // 2
    while half >= HGRN_DIAG:
        span = 2 * half
        pos = row % span
        first = (pos >= half) if rev else (pos < half)
        edge = half if rev else half - 1
        ref = _rows_of(cum, [j * span + edge for j in range(n // span)], span)
        qe = jnp.where(first, 0.0, q * jnp.exp(cum - ref))
        ke = jnp.where(first, k * jnp.exp(ref - cum), 0.0)
        a = lax.dot_general(qe.astype(BF16), ke.astype(BF16), _NT, preferred_element_type=F32)
        att = att + jnp.where(row // span == col // span, a, 0.0)
        half //= 2
    pos = row % HGRN_DIAG
    group_col = row - pos

    def row_of_group(x, j):
        x3 = x.reshape(n // HGRN_DIAG, HGRN_DIAG, n)
        return jnp.broadcast_to(x3[:, j:j + 1, :], x3.shape).reshape(n, n)

    for j in range(HGRN_DIAG):
        a = q * row_of_group(k, j) * jnp.exp(cum - row_of_group(cum, j))
        a = jnp.where((pos <= j) if rev else (pos >= j), a, 0.0)
        w = jnp.sum(a, axis=-1, keepdims=True)
        att = att + jnp.where(col == group_col + j, w, 0.0)
    o = jnp.dot(att.astype(BF16), v.astype(BF16), preferred_element_type=F32)
    o = o + jnp.dot((q * jnp.exp(cum)).astype(BF16), s0.astype(BF16), preferred_element_type=F32)
    last = cum[0:1, :] if rev else cum[n - 1:n, :]
    kd = (k * jnp.exp(last - cum)).astype(BF16)
    upd = lax.dot_general(kd, v.astype(BF16), _TN, preferred_element_type=F32)
    keep = jnp.sum(jnp.where(row == col, jnp.exp(last), 0.0), axis=-1, keepdims=True)
    return o, keep * s0 + upd


def _hgrn_kernel(qf_ref, if_ref, ff_ref, qb_ref, ib_ref, fb_ref, lbf_ref, lbb_ref, s0_ref,
                 of_ref, ob_ref, sfin_ref, state_ref, *, step_info):
    from_zero, c, nblk = step_info(pl.program_id(0))[:3]

    @pl.when((c == 0) & from_zero)
    def _():
        state_ref[...] = jnp.zeros_like(state_ref)

    @pl.when((c == 0) & jnp.logical_not(from_zero))
    def _():
        state_ref[...] = s0_ref[...]

    for rev, (q_ref, i_ref, f_ref, lb_ref, o_ref) in enumerate(
            ((qf_ref, if_ref, ff_ref, lbf_ref, of_ref), (qb_ref, ib_ref, fb_ref, lbb_ref, ob_ref))):
        for h in range(N_HEADS):
            q = q_ref[:, _head(h)]
            q = q * _sigmoid(q)
            logf = _log_forget_k(f_ref[:, _head(h)], lb_ref[:, _head(h)])
            k = 1.0 - jnp.exp(logf)
            o, s1 = _hgrn_block(q, k, i_ref[:, _head(h)], logf, state_ref[rev, h], bool(rev))
            o_ref[:, _head(h)] = o
            state_ref[rev, h] = s1

    @pl.when((c == nblk - 1) & from_zero)
    def _():
        sfin_ref[...] = state_ref[...]


def hgrn2_scan(proj, lb_f, lb_b, state_in, layer, n_zero, len_zero, n_init, len_init):
    blk = HGRN_BLOCK
    nz, ni = len_zero // blk, len_init // blk
    steps_zero = n_zero * nz

    def step_info(s):
        from_zero = s < steps_zero
        s2 = s - steps_zero
        b = jnp.where(from_zero, s // nz, s2 // ni)
        c = jnp.where(from_zero, s % nz, s2 % ni)
        nblk = jnp.where(from_zero, nz, ni)
        base = jnp.where(from_zero, b * nz, steps_zero + b * ni)
        return from_zero, c, nblk, b, base

    def fwd(group):
        def index(s):
            _, c, _, _, base = step_info(s)
            return (base + c, group)
        return pl.BlockSpec((blk, MIX_W), index)

    def bwd(group):
        def index(s):
            _, c, nblk, _, base = step_info(s)
            return (base + nblk - 1 - c, group)
        return pl.BlockSpec((blk, MIX_W), index)

    def init_index(s):
        from_zero, _, _, b, _ = step_info(s)
        return (jnp.where(from_zero, 0, b), layer, 0, 0, 0, 0)

    def final_index(s):
        from_zero, _, _, b, _ = step_info(s)
        return (jnp.where(from_zero, b, n_zero - 1), 0, 0, 0, 0)

    t = proj.shape[0]
    state_block = (None, 2, N_HEADS, HEAD_DIM, HEAD_DIM)
    lb_spec = pl.BlockSpec((1, MIX_W), lambda s: (0, 0))
    return pl.pallas_call(
        functools.partial(_hgrn_kernel, step_info=step_info),
        grid=(steps_zero + n_init * ni,),
        in_specs=[fwd(5), fwd(6), fwd(7), bwd(5), bwd(6), bwd(8), lb_spec, lb_spec,
                  pl.BlockSpec((None, None, 2, N_HEADS, HEAD_DIM, HEAD_DIM), init_index)],
        out_specs=[fwd(0), bwd(0), pl.BlockSpec(state_block, final_index)],
        out_shape=[jax.ShapeDtypeStruct((t, MIX_W), F32), jax.ShapeDtypeStruct((t, MIX_W), F32),
                   jax.ShapeDtypeStruct((n_zero, 2, N_HEADS, HEAD_DIM, HEAD_DIM), F32)],
        scratch_shapes=[pltpu.VMEM((2, N_HEADS, HEAD_DIM, HEAD_DIM), F32)],
        compiler_params=_params("arbitrary"),
        name="hgrn_scan",
    )(proj, proj, proj, proj, proj, proj, lb_f.reshape(1, MIX_W), lb_b.reshape(1, MIX_W), state_in)


def _hgrn_out_kernel(of_ref, ob_ref, g_ref, ng_ref, o_ref):
    for h in range(N_HEADS):
        o = of_ref[:, _head(h)] + ob_ref[:, _head(h)]
        o = o * lax.rsqrt(jnp.mean(o * o, axis=-1, keepdims=True) + EPS)
        g = g_ref[:, _head(h)]
        o_ref[:, _head(h)] = (o * ng_ref[:, _head(h)] * (g * _sigmoid(g))).astype(o_ref.dtype)


def hgrn2_output(o_f, o_b, proj, norm_g, layer, tm=512):
    t = o_f.shape[0]
    depth = norm_g.shape[0]
    tile = pl.BlockSpec((tm, MIX_W), lambda i: (i, 0))
    return pl.pallas_call(
        _hgrn_out_kernel,
        grid=(t // tm,),
        in_specs=[tile, tile, pl.BlockSpec((tm, MIX_W), lambda i: (i, 9)),
                  pl.BlockSpec((None, 1, MIX_W), lambda i: (layer, 0, 0))],
        out_specs=tile,
        out_shape=jax.ShapeDtypeStruct((t, MIX_W), BF16),
        compiler_params=_params("arbitrary"),
        name="hgrn_out",
    )(o_f, o_b, proj, norm_g.reshape(depth, 1, MIX_W))


def kernel(x_prompt, x_sample, cache_k, cache_v, state_hgrn, c, c_ctx, w_mod, b_mod, norm1_g, norm2_g, w_in, na_rpb, sgu_w, sgu_b, hgrn_lb, hgrn_norm_g, pool_w, pool_scale, w_out, router_w, router_b, exp_w_gu, exp_b_gu, exp_w_dn, exp_b_dn, final_norm_g):
    bp, sp, d = x_prompt.shape
    bs, ss, _ = x_sample.shape
    depth = w_mod.shape[0]
    tp, ts = bp * sp, bs * ss

    cond = jnp.zeros((8, d), F32).at[0].set(c_ctx).at[1:1 + bs].set(c)
    mods = adaln_all(cond, w_mod, b_mod).reshape(depth * 8 * N_MOD, 1, d)

    def mod_of_tile(tile_rows):
        prompt_tiles = tp // tile_rows
        per_seq = ss // tile_rows
        return lambda i: jnp.where(i < prompt_tiles, 0, 1 + (i - prompt_tiles) // per_seq)

    lb_p = jax.nn.softmax(hgrn_lb, axis=1)
    lb_all = jnp.maximum(jnp.cumsum(lb_p, axis=1) - lb_p[:, :1], 0.0)

    x = jnp.concatenate([x_prompt.reshape(tp, d), x_sample.reshape(ts, d)], axis=0)
    g1 = norm1_g.reshape(depth, 1, d)
    g2 = norm2_g.reshape(depth, 1, d)
    bias_table = na_bias_table(na_rpb)
    cache_k = cache_k.reshape(bs, depth, cache_k.shape[2], MIX_W)
    cache_v = cache_v.reshape(bs, depth, cache_v.shape[2], MIX_W)
    new_k, new_v, new_s = [], [], []
    h = norm_mod(x, g1, mods, 0, 1, 0, mod_of_tile(ROW_TILE))
    for l in range(depth):
        last = l == depth - 1
        proj = proj_in(h, w_in, l)
        o_a = jnp.concatenate([context_attention(proj, bp, sp),
                               neighbourhood_attention(proj, cache_k, cache_v, bias_table, l, tp, bs, ss)])
        o_b = spatial_gating(proj, sgu_w, sgu_b, l)
        o_f, o_r, s_l = hgrn2_scan(proj, lb_all[0, l], lb_all[1, l], state_hgrn, l, bp, sp, bs, ss)
        o_c = hgrn2_output(o_f, o_r, proj, hgrn_norm_g, l)
        o_d = jnp.concatenate([multiscale_pool(proj, pool_w, pool_scale, l, 0, bp, sp),
                               multiscale_pool(proj, pool_w, pool_scale, l, tp, bs, ss)])
        new_k.append(proj[:tp, MIX_W:2 * MIX_W].reshape(bp, sp, N_HEADS, HEAD_DIM))
        new_v.append(proj[:tp, 2 * MIX_W:3 * MIX_W].reshape(bp, sp, N_HEADS, HEAD_DIM))
        new_s.append(s_l)
        x = proj_out_residual([o_a, o_b, o_c, o_d], w_out, x, mods, l, mod_of_tile(1024))
        x, h = moe_layer(x, g2, mods, router_w, router_b, exp_w_gu, exp_b_gu, exp_w_dn, exp_b_dn, l,
                         mod_of_tile(ROW_TILE), final_norm_g.reshape(1, 1, d) if last else g1,
                         tp if last else None)
    return (x.reshape(bp, sp, d), h.reshape(bs, ss, d),
            jnp.stack(new_k, axis=1), jnp.stack(new_v, axis=1), jnp.stack(new_s, axis=1))
```

```python
import functools
import math

import numpy as np
import jax
import jax.numpy as jnp
from jax import lax
from jax.experimental import pallas as pl
from jax.experimental.pallas import tpu as pltpu

F32 = jnp.float32
BF16 = jnp.bfloat16

N_MOD = 6
N_PROJ = 11
MIX_W = 512
HEAD_DIM = 128
N_HEADS = 4
GRID_W = 64
NA_WIN_ROWS = 8
NA_WIN_COLS = 16
POOL_WINDOWS = (2, 4, 8, 16)
N_EXPERTS = 32
TOP_K = 4
SWIGLU_ALPHA = 1.702
SWIGLU_LIMIT = 7.0
EPS = 1e-6
LB_TINY = 1e-30
NEG_BIG = -1e30
HGRN_CHUNK = 16

LANES = 128
ROW_TILE = 256
VMEM_LIMIT = 56 * 1024 * 1024


def _params(*sem, vmem=VMEM_LIMIT):
    return pltpu.CompilerParams(dimension_semantics=sem, vmem_limit_bytes=vmem)


def _sigmoid(x):
    return 1.0 / (1.0 + jnp.exp(-x))


def _adaln_kernel(c_ref, w_ref, b_ref, o_ref):
    c = c_ref[...]
    s = (c * _sigmoid(c)).astype(BF16)
    o_ref[...] = jnp.dot(s, w_ref[...].astype(BF16), preferred_element_type=F32) + b_ref[...]


def adaln_all(cond, w_mod, b_mod, tn=1024):
    depth, d, n = w_mod.shape
    rows = cond.shape[0]
    tn = min(tn, n)
    return pl.pallas_call(
        _adaln_kernel,
        grid=(depth, n // tn),
        in_specs=[
            pl.BlockSpec((rows, d), lambda l, j: (0, 0)),
            pl.BlockSpec((None, d, tn), lambda l, j: (l, 0, j)),
            pl.BlockSpec((None, 1, tn), lambda l, j: (l, 0, j)),
        ],
        out_specs=pl.BlockSpec((None, rows, tn), lambda l, j: (l, 0, j)),
        out_shape=jax.ShapeDtypeStruct((depth, rows, n), F32),
        compiler_params=_params("arbitrary", "arbitrary"),
        name="adaln",
    )(cond, w_mod, b_mod.reshape(depth, 1, n))


def _rms(x):
    return x * lax.rsqrt(jnp.mean(x * x, axis=-1, keepdims=True) + EPS)


def _normmod_kernel(x_ref, g_ref, sc_ref, sh_ref, o_ref):
    h = (_rms(x_ref[...]) * g_ref[...]) * (1.0 + sc_ref[...]) + sh_ref[...]
    o_ref[...] = h.astype(o_ref.dtype)


def _mod_spec(d, layer, which, mod_of_tile):
    return pl.BlockSpec((None, 1, d), lambda i, *_: ((layer * 8 + mod_of_tile(i)) * N_MOD + which, 0, 0))


def norm_mod(x, gain, mods, layer, which_scale, which_shift, mod_of_tile, tm=ROW_TILE):
    t, d = x.shape
    return pl.pallas_call(
        _normmod_kernel,
        grid=(t // tm,),
        in_specs=[
            pl.BlockSpec((tm, d), lambda i: (i, 0)),
            pl.BlockSpec((None, 1, d), lambda i: (layer, 0, 0)),
            _mod_spec(d, layer, which_scale, mod_of_tile),
            _mod_spec(d, layer, which_shift, mod_of_tile),
        ],
        out_specs=pl.BlockSpec((tm, d), lambda i: (i, 0)),
        out_shape=jax.ShapeDtypeStruct((t, d), BF16),
        compiler_params=_params("arbitrary"),
        name="norm_mod",
    )(x, gain, mods, mods)


def _mm_kernel(x_ref, w_ref, o_ref):
    o_ref[...] = jnp.dot(x_ref[...], w_ref[...].astype(BF16),
                         preferred_element_type=F32).astype(o_ref.dtype)


def proj_in(h, w_in, layer, tm=2048, tn=512):
    t, d = h.shape
    n = w_in.shape[-1]
    tm, tn = min(tm, t), min(tn, n)
    return pl.pallas_call(
        _mm_kernel,
        grid=(t // tm, n // tn),
        in_specs=[
            pl.BlockSpec((tm, d), lambda i, j: (i, 0)),
            pl.BlockSpec((None, d, tn), lambda i, j: (layer, 0, j)),
        ],
        out_specs=pl.BlockSpec((tm, tn), lambda i, j: (i, j)),
        out_shape=jax.ShapeDtypeStruct((t, n), F32),
        compiler_params=_params("arbitrary", "arbitrary"),
        name="proj_in",
    )(h, w_in)


def _proj_out_kernel(a_ref, b_ref, c_ref, d_ref, w_ref, x_ref, g_ref, o_ref, wbf_ref):
    @pl.when(pl.program_id(1) == 0)
    def _():
        wbf_ref[...] = w_ref[...].astype(BF16)

    k = a_ref.shape[1]
    acc = jnp.dot(a_ref[...], wbf_ref[0:k, :], preferred_element_type=F32)
    acc += jnp.dot(b_ref[...], wbf_ref[k:2 * k, :], preferred_element_type=F32)
    acc += jnp.dot(c_ref[...], wbf_ref[2 * k:3 * k, :], preferred_element_type=F32)
    acc += jnp.dot(d_ref[...], wbf_ref[3 * k:4 * k, :], preferred_element_type=F32)
    o_ref[...] = x_ref[...] + g_ref[...] * acc


def proj_out_residual(parts, w_out, x, mods, layer, mod_of_tile, tm=1024, tn=512):
    t, d = x.shape
    k = parts[0].shape[1]
    tm, tn = min(tm, t), min(tn, d)
    part_spec = pl.BlockSpec((tm, k), lambda j, i: (i, 0))
    return pl.pallas_call(
        _proj_out_kernel,
        grid=(d // tn, t // tm),
        in_specs=[part_spec] * 4 + [
            pl.BlockSpec((None, d, tn), lambda j, i: (layer, 0, j)),
            pl.BlockSpec((tm, tn), lambda j, i: (i, j)),
            pl.BlockSpec((None, 1, tn),
                         lambda j, i: ((layer * 8 + mod_of_tile(i)) * N_MOD + 2, 0, j)),
        ],
        out_specs=pl.BlockSpec((tm, tn), lambda j, i: (i, j)),
        out_shape=jax.ShapeDtypeStruct((t, d), F32),
        scratch_shapes=[pltpu.VMEM((d, tn), BF16)],
        compiler_params=_params("arbitrary", "arbitrary"),
        name="proj_out",
    )(*parts, w_out, x, mods)


def _pack_bf16_halves(x):
    n = x.shape[1] // 2
    lo = lax.bitcast_convert_type(x[:, :n].astype(BF16).astype(F32), jnp.uint32)
    hi = lax.bitcast_convert_type(x[:, n:].astype(BF16).astype(F32), jnp.uint32)
    return (lo >> 16) | (hi & jnp.uint32(0xFFFF0000))


def _unpack_bf16_halves(w):
    lo = lax.bitcast_convert_type(w << 16, F32).astype(BF16)
    hi = lax.bitcast_convert_type(w & jnp.uint32(0xFFFF0000), F32).astype(BF16)
    return jnp.concatenate([lo, hi], axis=1)


def _router_kernel(x_ref, g_ref, sc_ref, sh_ref, rw_ref, rb_ref,
                   h_ref, idx_ref, gate_ref, rank_ref, cnt_ref, carry_ref):
    i = pl.program_id(0)

    @pl.when(i == 0)
    def _():
        carry_ref[...] = jnp.zeros_like(carry_ref)

    h = (_rms(x_ref[...]) * g_ref[...]) * (1.0 + sc_ref[...]) + sh_ref[...]
    packed = _pack_bf16_halves(h)
    s_rows = packed.shape[1] // LANES
    for s in range(s_rows):
        h_ref[pl.ds(s, h.shape[0], stride=s_rows), :] = packed[:, s * LANES:(s + 1) * LANES]
    h_hi = h.astype(BF16)
    h_lo = (h - h_hi.astype(F32)).astype(BF16)
    logits = (jnp.dot(h_hi, rw_ref[0], preferred_element_type=F32)
              + jnp.dot(h_lo, rw_ref[0], preferred_element_type=F32)
              + jnp.dot(h_hi, rw_ref[1], preferred_element_type=F32)) + rb_ref[...]
    tm = logits.shape[0]
    lane = lax.broadcasted_iota(jnp.int32, (tm, LANES), 1)
    vals, hots = [], []
    idx_out = jnp.zeros((tm, LANES), jnp.int32)
    for j in range(TOP_K):
        m = jnp.max(logits, axis=-1, keepdims=True)
        idx = jnp.min(jnp.where(logits == m, lane, LANES), axis=-1, keepdims=True)
        hot = lane == idx
        vals.append(m)
        hots.append(hot)
        idx_out = jnp.where(lane == j, idx, idx_out)
        logits = jnp.where(hot, -jnp.inf, logits)
    exps = [jnp.exp(v - vals[0]) for v in vals]
    denom = exps[0] + exps[1] + exps[2] + exps[3]
    gate_out = jnp.zeros((tm, LANES), F32)
    for j in range(TOP_K):
        gate_out = jnp.where(lane == j, exps[j] / denom, gate_out)
    chosen = (hots[0] | hots[1] | hots[2] | hots[3])
    chosen_f = jnp.where(chosen, 1.0, 0.0)
    row = lax.broadcasted_iota(jnp.int32, (tm, tm), 0)
    col = lax.broadcasted_iota(jnp.int32, (tm, tm), 1)
    before = jnp.where(col < row, 1.0, 0.0).astype(BF16)
    base = carry_ref[...] + jnp.dot(before, chosen_f.astype(BF16), preferred_element_type=F32)
    rank_out = jnp.zeros((tm, LANES), F32)
    for j in range(TOP_K):
        r = jnp.sum(jnp.where(hots[j], base, 0.0), axis=-1, keepdims=True)
        rank_out = jnp.where(lane == j, r, rank_out)
    carry_ref[...] += jnp.sum(chosen_f, axis=0, keepdims=True)
    idx_ref[...] = idx_out
    gate_ref[...] = gate_out
    rank_ref[...] = rank_out.astype(jnp.int32)
    cnt_ref[...] = carry_ref[...].astype(jnp.int32)


def route(x, gain, mods, router_w, router_b, layer, mod_of_tile, tm=ROW_TILE):
    t, d = x.shape
    ne = router_w.shape[-1]
    rw = jnp.pad(router_w[layer], ((0, 0), (0, LANES - ne)))
    rw_hi = rw.astype(BF16)
    rw = jnp.stack([rw_hi, (rw - rw_hi.astype(F32)).astype(BF16)])
    rb = jnp.pad(router_b[layer], (0, LANES - ne), constant_values=-jnp.inf).reshape(1, LANES)
    tile = pl.BlockSpec((tm, LANES), lambda i: (i, 0))
    return pl.pallas_call(
        _router_kernel,
        grid=(t // tm,),
        in_specs=[
            pl.BlockSpec((tm, d), lambda i: (i, 0)),
            pl.BlockSpec((None, 1, d), lambda i: (layer, 0, 0)),
            _mod_spec(d, layer, 4, mod_of_tile),
            _mod_spec(d, layer, 3, mod_of_tile),
            pl.BlockSpec((2, d, LANES), lambda i: (0, 0, 0)),
            pl.BlockSpec((1, LANES), lambda i: (0, 0)),
        ],
        out_specs=[pl.BlockSpec((tm * (d // 2 // LANES), LANES), lambda i: (i, 0)), tile, tile, tile,
                   pl.BlockSpec((1, LANES), lambda i: (0, 0))],
        out_shape=[jax.ShapeDtypeStruct((t * (d // 2 // LANES), LANES), jnp.uint32),
                   jax.ShapeDtypeStruct((t, LANES), jnp.int32),
                   jax.ShapeDtypeStruct((t, LANES), F32),
                   jax.ShapeDtypeStruct((t, LANES), jnp.int32),
                   jax.ShapeDtypeStruct((1, LANES), jnp.int32)],
        scratch_shapes=[pltpu.VMEM((1, LANES), F32)],
        compiler_params=_params("arbitrary"),
        name="route",
    )(x, gain, mods, mods, rw, rb)


DMA_ISSUE_UNROLL = 8


def _dispatch_kernel(dest_ref, fill_ref, nact_ref, h_ref, xs_hbm, zero_ref, sem, zsem, *, n_blocks, s_rows):
    i = pl.program_id(0)
    tm = h_ref.shape[0] // s_rows

    def row_copy(src_ref, r, dst, s):
        return pltpu.make_async_copy(src_ref.at[pl.ds(pl.multiple_of(r * s_rows, s_rows), s_rows), :],
                                     xs_hbm.at[pl.ds(pl.multiple_of(dst * s_rows, s_rows), s_rows), :], s)

    def issue(r, carry):
        for j in range(TOP_K):
            row_copy(h_ref, r, dest_ref[(i * tm + r) * TOP_K + j], sem).start(priority=j % 2)
        return carry

    lax.fori_loop(0, tm, issue, 0, unroll=DMA_ISSUE_UNROLL // 2)

    @pl.when(i == pl.num_programs(0) - 1)
    def _():
        zero_ref[...] = jnp.zeros_like(zero_ref)
        n_experts = fill_ref.shape[0] // 2

        def fill_expert(e, carry):
            first, count = fill_ref[2 * e], fill_ref[2 * e + 1]

            def start(r, c):
                row_copy(zero_ref, 0, first + r, zsem).start()
                return c

            def wait(r, c):
                row_copy(zero_ref, 0, first + r, zsem).wait()
                return c

            lax.fori_loop(0, count, start, 0)
            lax.fori_loop(0, count, wait, 0)
            return carry

        lax.fori_loop(0, n_experts, fill_expert, 0)

        def block_copy(b):
            rows = tm * s_rows
            return pltpu.make_async_copy(zero_ref, xs_hbm.at[pl.ds(pl.multiple_of(b * rows, rows), rows), :],
                                         zsem)

        def start_block(b, c):
            block_copy(b).start()
            return c

        def wait_block(b, c):
            block_copy(b).wait()
            return c

        lax.fori_loop(nact_ref[0], n_blocks, start_block, 0)
        lax.fori_loop(nact_ref[0], n_blocks, wait_block, 0)

    for j in range(TOP_K):
        pltpu.make_async_copy(h_ref, xs_hbm.at[pl.ds(0, tm * s_rows), :], sem).wait()


def dispatch_rows(h, n_tokens, dest, fill, n_active, n_blocks, tm=ROW_TILE):
    s_rows = h.shape[0] // n_tokens
    block = (tm * s_rows, h.shape[1])
    grid_spec = pltpu.PrefetchScalarGridSpec(
        num_scalar_prefetch=3,
        grid=(n_tokens // tm,),
        in_specs=[pl.BlockSpec(block, lambda i, *_: (i, 0))],
        out_specs=pl.BlockSpec(memory_space=pl.ANY),
        scratch_shapes=[pltpu.VMEM(block, h.dtype), pltpu.SemaphoreType.DMA, pltpu.SemaphoreType.DMA],
    )
    return pl.pallas_call(
        functools.partial(_dispatch_kernel, n_blocks=n_blocks, s_rows=s_rows),
        grid_spec=grid_spec,
        out_shape=jax.ShapeDtypeStruct((n_blocks * block[0], block[1]), h.dtype),
        compiler_params=_params("arbitrary"),
        name="moe_dispatch",
    )(dest, fill, n_active, h)


def _expert_changed(i, be_ref):
    return (i == 0) | (be_ref[i] != be_ref[jnp.maximum(i - 1, 0)])


def expert_runs(counts, block_expert):
    ne = counts.shape[0]
    ids = jnp.arange(ne, dtype=jnp.int32)
    present = counts > 0
    run_of_expert = jnp.cumsum(present.astype(jnp.int32)) - 1
    later = jnp.where((ids[None, :] > ids[:, None]) & present[None, :], ids[None, :], ne)
    first_present = jnp.min(jnp.where(present, ids, ne))
    nxt = jnp.min(later, axis=1)
    next_of_expert = jnp.where(nxt == ne, first_present, nxt).astype(jnp.int32)
    n_runs = jnp.sum(present.astype(jnp.int32)).reshape(1)
    return run_of_expert[block_expert], next_of_expert[block_expert], n_runs


def _stream_weights(c, i, be_ref, run_ref, next_ref, nruns_ref, copies, casts):
    n_runs = nruns_ref[0]
    g = c * n_runs + run_ref[i]
    slot = g % 2
    weight_queue = 1

    @pl.when(g == 0)
    def _():
        for cp in copies(be_ref[i], c, slot):
            cp.start(priority=weight_queue)

    for cp in copies(be_ref[i], c, slot):
        cp.wait()

    @pl.when(g + 1 < pl.num_programs(0) * n_runs)
    def _():
        wraps = run_ref[i] == n_runs - 1
        for cp in copies(next_ref[i], jnp.where(wraps, c + 1, c), 1 - slot):
            cp.start(priority=weight_queue)

    casts(slot)


def _gmm_up_kernel(be_ref, run_ref, next_ref, nruns_ref, nact_ref, x_ref, w_hbm, bg_ref, bu_ref, o_ref,
                   wg_f32, wu_f32, wg_bf, wu_bf, sem, *, layer, tn, dff):
    c = pl.program_id(0)
    i = pl.program_id(1)

    def copies(e, cc, slot):
        col = pl.multiple_of(cc * tn, tn)
        return [pltpu.make_async_copy(w_hbm.at[layer, e, :, pl.ds(col, tn)], wg_f32.at[slot], sem.at[slot]),
                pltpu.make_async_copy(w_hbm.at[layer, e, :, pl.ds(dff + col, tn)], wu_f32.at[slot],
                                      sem.at[slot])]

    def casts(slot):
        wg_bf[...] = wg_f32[slot].astype(BF16)
        wu_bf[...] = wu_f32[slot].astype(BF16)

    @pl.when(i < nact_ref[0])
    def _():
        @pl.when(_expert_changed(i, be_ref))
        def _():
            _stream_weights(c, i, be_ref, run_ref, next_ref, nruns_ref, copies, casts)

        s_rows = x_ref.shape[0] // o_ref.shape[0]
        words = [x_ref[pl.ds(s, o_ref.shape[0], stride=s_rows), :] for s in range(s_rows)]
        x = _unpack_bf16_halves(jnp.concatenate(words, axis=1))
        g = jnp.dot(x, wg_bf[...], preferred_element_type=F32) + bg_ref[...]
        u = jnp.dot(x, wu_bf[...], preferred_element_type=F32) + bu_ref[...]
        gate = jnp.minimum(g, SWIGLU_LIMIT)
        up = jnp.clip(u, -SWIGLU_LIMIT, SWIGLU_LIMIT)
        act = (up + 1.0) * gate * _sigmoid(gate * SWIGLU_ALPHA)
        o_ref[...] = act.astype(o_ref.dtype)

    @pl.when(i >= nact_ref[0])
    def _():
        o_ref[...] = jnp.zeros_like(o_ref)


def _row_block(i, na):
    return jnp.minimum(i, na[0] - 1)


def gmm_up(xs, w_gu, b_gu, block_expert, runs, n_active, layer, tm=ROW_TILE, tn=1024):
    d = w_gu.shape[-2]
    s_rows = d // 2 // xs.shape[1]
    p = xs.shape[0] // s_rows
    dff = w_gu.shape[-1] // 2
    tn = min(tn, dff)
    nb, nc = p // tm, dff // tn
    depth, ne = b_gu.shape[:2]
    b4 = b_gu.reshape(depth, ne, 1, 2 * dff)
    grid_spec = pltpu.PrefetchScalarGridSpec(
        num_scalar_prefetch=5,
        grid=(nc, nb),
        in_specs=[
            pl.BlockSpec((tm * s_rows, xs.shape[1]),
                         lambda c, i, be, rn, nx, nr, na: (_row_block(i, na), 0)),
            pl.BlockSpec(memory_space=pl.ANY),
            pl.BlockSpec((None, None, 1, tn),
                         lambda c, i, be, rn, nx, nr, na: (layer, be[_row_block(i, na)], 0, c)),
            pl.BlockSpec((None, None, 1, tn),
                         lambda c, i, be, rn, nx, nr, na: (layer, be[_row_block(i, na)], 0, nc + c)),
        ],
        out_specs=pl.BlockSpec((tm, tn), lambda c, i, *_: (i, c)),
        scratch_shapes=[pltpu.VMEM((2, d, tn), F32), pltpu.VMEM((2, d, tn), F32),
                        pltpu.VMEM((d, tn), BF16), pltpu.VMEM((d, tn), BF16),
                        pltpu.SemaphoreType.DMA((2,))],
    )
    return pl.pallas_call(
        functools.partial(_gmm_up_kernel, layer=layer, tn=tn, dff=dff),
        grid_spec=grid_spec,
        out_shape=jax.ShapeDtypeStruct((p, dff), BF16),
        compiler_params=_params("arbitrary", "arbitrary"),
        name="moe_up",
    )(block_expert, *runs, n_active, xs, w_gu, b4, b4)


def _gmm_down_kernel(be_ref, run_ref, next_ref, nruns_ref, nact_ref, a_ref, w_hbm, b_ref, o_ref,
                     w_f32, w_bf, sem, *, layer, tn):
    c = pl.program_id(0)
    i = pl.program_id(1)

    def copies(e, cc, slot):
        col = pl.multiple_of(cc * tn, tn)
        return [pltpu.make_async_copy(w_hbm.at[layer, e, :, pl.ds(col, tn)], w_f32.at[slot], sem.at[slot])]

    def casts(slot):
        w_bf[...] = w_f32[slot].astype(BF16)

    @pl.when(i < nact_ref[0])
    def _():
        @pl.when(_expert_changed(i, be_ref))
        def _():
            _stream_weights(c, i, be_ref, run_ref, next_ref, nruns_ref, copies, casts)

        o_ref[...] = jnp.dot(a_ref[...], w_bf[...], preferred_element_type=F32) + b_ref[...]

    @pl.when(i >= nact_ref[0])
    def _():
        o_ref[...] = jnp.zeros_like(o_ref)


def gmm_down(act, w_dn, b_dn, block_expert, runs, n_active, layer, tm=ROW_TILE, tn=2048):
    p, dff = act.shape
    d = w_dn.shape[-1]
    tn = min(tn, d)
    nb, nc = p // tm, d // tn
    depth, ne = b_dn.shape[:2]
    b4 = b_dn.reshape(depth, ne, 1, d)
    grid_spec = pltpu.PrefetchScalarGridSpec(
        num_scalar_prefetch=5,
        grid=(nc, nb),
        in_specs=[
            pl.BlockSpec((tm, dff), lambda c, i, be, rn, nx, nr, na: (_row_block(i, na), 0)),
            pl.BlockSpec(memory_space=pl.ANY),
            pl.BlockSpec((None, None, 1, tn),
                         lambda c, i, be, rn, nx, nr, na: (layer, be[_row_block(i, na)], 0, c)),
        ],
        out_specs=pl.BlockSpec((tm, tn), lambda c, i, *_: (i, c)),
        scratch_shapes=[pltpu.VMEM((2, dff, tn), F32), pltpu.VMEM((dff, tn), BF16),
                        pltpu.SemaphoreType.DMA((2,))],
    )
    return pl.pallas_call(
        functools.partial(_gmm_down_kernel, layer=layer, tn=tn),
        grid_spec=grid_spec,
        out_shape=jax.ShapeDtypeStruct((p, d), F32),
        compiler_params=_params("arbitrary", "arbitrary"),
        name="moe_down",
    )(block_expert, *runs, n_active, act, w_dn, b4)


def _combine_kernel(dest_ref, y_hbm, x_ref, gate_ref, g2_ref, ng_ref, sc_ref, sh_ref, o1_ref, o2_ref,
                    buf_ref, sem, *, first_tiles):
    i = pl.program_id(0)
    tm = x_ref.shape[0]

    def issue_block(blk):
        slot = blk % 2

        def issue(r, carry):
            for j in range(TOP_K):
                src = dest_ref[(blk * tm + r) * TOP_K + j]
                pltpu.make_async_copy(y_hbm.at[pl.ds(src, 1), :],
                                      buf_ref.at[slot, j, pl.ds(r, 1), :],
                                      sem.at[slot]).start(priority=j % 2)
            return carry

        lax.fori_loop(0, tm, issue, 0, unroll=DMA_ISSUE_UNROLL // 2)

    @pl.when(i == 0)
    def _():
        issue_block(i)

    @pl.when(i + 1 < pl.num_programs(0))
    def _():
        issue_block(i + 1)

    slot = i % 2
    for j in range(TOP_K):
        pltpu.make_async_copy(y_hbm.at[pl.ds(0, tm), :], buf_ref.at[slot, j], sem.at[slot]).wait()
    gates = gate_ref[...]
    acc = gates[:, 0:1] * buf_ref[slot, 0]
    for j in range(1, TOP_K):
        acc += gates[:, j:j + 1] * buf_ref[slot, j]
    x_new = x_ref[...] + g2_ref[...] * acc
    normed = _rms(x_new) * ng_ref[...]
    if first_tiles is None:
        o1_ref[...] = x_new
        o2_ref[...] = (normed * (1.0 + sc_ref[...]) + sh_ref[...]).astype(o2_ref.dtype)
    else:
        @pl.when(i < first_tiles)
        def _():
            o1_ref[...] = normed

        @pl.when(i >= first_tiles)
        def _():
            o2_ref[...] = normed


def combine_residual(y, dest, gates, x, mods, layer, mod_of_tile, next_gain, n_first, tm=ROW_TILE):
    t, d = x.shape
    last = n_first is not None
    tile = lambda i, dst: (i, 0)
    if last:
        first_tiles = n_first // tm
        out_specs = [pl.BlockSpec((tm, d), lambda i, dst: (jnp.minimum(i, first_tiles - 1), 0)),
                     pl.BlockSpec((tm, d), lambda i, dst: (jnp.maximum(i - first_tiles, 0), 0))]
        out_shape = [jax.ShapeDtypeStruct((n_first, d), F32), jax.ShapeDtypeStruct((t - n_first, d), F32)]
        gain_spec = pl.BlockSpec((None, 1, d), lambda i, dst: (0, 0, 0))
        mod_layer = layer
    else:
        first_tiles = None
        out_specs = [pl.BlockSpec((tm, d), tile), pl.BlockSpec((tm, d), tile)]
        out_shape = [jax.ShapeDtypeStruct((t, d), F32), jax.ShapeDtypeStruct((t, d), BF16)]
        gain_spec = pl.BlockSpec((None, 1, d), lambda i, dst: (layer + 1, 0, 0))
        mod_layer = layer + 1
    grid_spec = pltpu.PrefetchScalarGridSpec(
        num_scalar_prefetch=1,
        grid=(t // tm,),
        in_specs=[
            pl.BlockSpec(memory_space=pl.ANY),
            pl.BlockSpec((tm, d), tile),
            pl.BlockSpec((tm, LANES), tile),
            _mod_spec(d, layer, 5, mod_of_tile),
            gain_spec,
            _mod_spec(d, mod_layer, 1, mod_of_tile),
            _mod_spec(d, mod_layer, 0, mod_of_tile),
        ],
        out_specs=out_specs,
        scratch_shapes=[pltpu.VMEM((2, TOP_K, tm, d), F32), pltpu.SemaphoreType.DMA((2,))],
    )
    return pl.pallas_call(
        functools.partial(_combine_kernel, first_tiles=first_tiles),
        grid_spec=grid_spec,
        out_shape=out_shape,
        compiler_params=_params("arbitrary"),
        name="moe_combine",
    )(dest, y, x, gates, mods, next_gain, mods, mods)


def moe_layer(x, gain, mods, router_w, router_b, w_gu, b_gu, w_dn, b_dn, layer, mod_of_tile,
              next_gain, n_first):
    t, d = x.shape
    ne = router_w.shape[-1]
    n_blocks = t * TOP_K // ROW_TILE + ne
    h, idx, gates, rank, counts = route(x, gain, mods, router_w, router_b, layer, mod_of_tile)
    counts = counts[0, :ne]
    padded = (counts + ROW_TILE - 1) // ROW_TILE * ROW_TILE
    ends = jnp.cumsum(padded)
    pstart = ends - padded
    e_idx = idx[:, :TOP_K]
    dest = (pstart[e_idx] + rank[:, :TOP_K]).reshape(t * TOP_K)
    block_start = jnp.arange(n_blocks, dtype=jnp.int32) * ROW_TILE
    block_expert = jnp.minimum(jnp.sum((ends[None, :] <= block_start[:, None]).astype(jnp.int32), axis=1),
                               ne - 1)
    n_active = (ends[-1:] // ROW_TILE).astype(jnp.int32)
    fill = jnp.stack([pstart + counts, padded - counts], axis=1).reshape(2 * ne).astype(jnp.int32)
    runs = expert_runs(counts, block_expert)
    xs = dispatch_rows(h, t, dest, fill, n_active, n_blocks)
    act = gmm_up(xs, w_gu, b_gu, block_expert, runs, n_active, layer)
    y = gmm_down(act, w_dn, b_dn, block_expert, runs, n_active, layer)
    return combine_residual(y, dest, gates, x, mods, layer, mod_of_tile, next_gain, n_first)


ATTN_SCALE = HEAD_DIM ** -0.5
_NT = (((1,), (1,)), ((), ()))
_TN = (((0,), (0,)), ((), ()))


def _head(h):
    return slice(h * HEAD_DIM, (h + 1) * HEAD_DIM)


def _ctx_attn_kernel(q_ref, k_ref, v_ref, o_ref):
    for h in range(N_HEADS):
        q = q_ref[:, _head(h)].astype(BF16)
        k = k_ref[:, _head(h)].astype(BF16)
        v = v_ref[:, _head(h)].astype(BF16)
        s = lax.dot_general(q, k, _NT, preferred_element_type=F32) * ATTN_SCALE
        e = jnp.exp(s - jnp.max(s, axis=-1, keepdims=True))
        p = e / jnp.sum(e, axis=-1, keepdims=True)
        o_ref[:, _head(h)] = jnp.dot(p.astype(BF16), v, preferred_element_type=F32).astype(o_ref.dtype)


def context_attention(proj, n_seq, seq_len):
    def spec(group):
        return pl.BlockSpec((seq_len, MIX_W), lambda b: (b, group))

    return pl.pallas_call(
        _ctx_attn_kernel,
        grid=(n_seq,),
        in_specs=[spec(0), spec(1), spec(2)],
        out_specs=pl.BlockSpec((seq_len, MIX_W), lambda b: (b, 0)),
        out_shape=jax.ShapeDtypeStruct((n_seq * seq_len, MIX_W), BF16),
        compiler_params=_params("arbitrary"),
        name="ctx_attn",
    )(proj, proj, proj)


def na_bias_table(rpb):
    depth, nh = rpb.shape[:2]
    qc = np.arange(GRID_W)[:, None]
    kc = np.arange(GRID_W)[None, :]
    d_col = np.clip(kc - qc + NA_WIN_COLS - 1, 0, 2 * NA_WIN_COLS - 2)
    c0 = np.clip(qc - NA_WIN_COLS // 2, 0, GRID_W - NA_WIN_COLS)
    inside = (kc >= c0) & (kc < c0 + NA_WIN_COLS)
    onehot = (d_col.reshape(-1)[None, :] == np.arange(2 * NA_WIN_COLS - 1)[:, None]).astype(np.float32)
    tt = jnp.einsum('lhij,jm->lhim', rpb, jnp.asarray(onehot), precision=lax.Precision.HIGHEST)
    tt = jnp.where(inside[None, None, None], tt.reshape(depth, nh, -1, GRID_W, GRID_W), NEG_BIG)
    return jnp.stack([jnp.concatenate([tt[:, :, i0 + kk] for kk in range(NA_WIN_ROWS)], axis=-1)
                      for i0 in range(NA_WIN_ROWS)], axis=2)


def _na_attn_kernel(q_ref, k_ref, v_ref, ck_ref, cv_ref, bias_ref, o_ref, *, rows):
    r = pl.program_id(1)
    r0 = jnp.clip(r - NA_WIN_ROWS // 2, 0, rows - NA_WIN_ROWS)
    start = pl.multiple_of(r0 * GRID_W, GRID_W)
    win = NA_WIN_ROWS * GRID_W
    for h in range(N_HEADS):
        q = q_ref[:, _head(h)].astype(BF16)
        kw = k_ref[pl.ds(start, win), _head(h)].astype(BF16)
        vw = v_ref[pl.ds(start, win), _head(h)].astype(BF16)
        ck = ck_ref[:, _head(h)].astype(BF16)
        cv = cv_ref[:, _head(h)].astype(BF16)
        s_loc = lax.dot_general(q, kw, _NT, preferred_element_type=F32) * ATTN_SCALE + bias_ref[h]
        s_ctx = lax.dot_general(q, ck, _NT, preferred_element_type=F32) * ATTN_SCALE
        m = jnp.maximum(jnp.max(s_loc, axis=-1, keepdims=True), jnp.max(s_ctx, axis=-1, keepdims=True))
        e_loc = jnp.exp(s_loc - m)
        e_ctx = jnp.exp(s_ctx - m)
        den = jnp.sum(e_loc, axis=-1, keepdims=True) + jnp.sum(e_ctx, axis=-1, keepdims=True)
        o = (jnp.dot((e_loc / den).astype(BF16), vw, preferred_element_type=F32)
             + jnp.dot((e_ctx / den).astype(BF16), cv, preferred_element_type=F32))
        o_ref[:, _head(h)] = o.astype(o_ref.dtype)


def neighbourhood_attention(proj, cache_k, cache_v, bias_table, layer, row0, n_seq, seq_len):
    rows = seq_len // GRID_W
    assert rows >= NA_WIN_ROWS and row0 % seq_len == 0
    ctx = cache_k.shape[2]
    ck, cv = cache_k, cache_v

    def offset_in_window(r):
        return jnp.clip(r - NA_WIN_ROWS // 2, 0, rows - NA_WIN_ROWS) - r + NA_WIN_ROWS - 1

    def seq_spec(group):
        return pl.BlockSpec((seq_len, MIX_W), lambda b, r: (row0 // seq_len + b, group))

    ctx_spec = pl.BlockSpec((None, None, ctx, MIX_W), lambda b, r: (b, layer, 0, 0))
    return pl.pallas_call(
        functools.partial(_na_attn_kernel, rows=rows),
        grid=(n_seq, rows),
        in_specs=[
            pl.BlockSpec((GRID_W, MIX_W), lambda b, r: (row0 // GRID_W + b * rows + r, 0)),
            seq_spec(1), seq_spec(2), ctx_spec, ctx_spec,
            pl.BlockSpec((None, N_HEADS, None, GRID_W, NA_WIN_ROWS * GRID_W),
                         lambda b, r: (layer, 0, offset_in_window(r), 0, 0)),
        ],
        out_specs=pl.BlockSpec((GRID_W, MIX_W), lambda b, r: (b * rows + r, 0)),
        out_shape=jax.ShapeDtypeStruct((n_seq * seq_len, MIX_W), BF16),
        compiler_params=_params("arbitrary", "arbitrary"),
        name="na_attn",
    )(proj, proj, proj, ck, cv, bias_table)


def _gelu_tanh(x):
    return 0.5 * x * (1.0 + jnp.tanh(math.sqrt(2.0 / math.pi) * (x + 0.044715 * (x * x * x))))


def _sgu_kernel(u_ref, v_ref, w_ref, b_ref, o_ref):
    for g in range(N_HEADS):
        u = _gelu_tanh(u_ref[:, _head(g)])
        v = _gelu_tanh(v_ref[:, _head(g)])
        vn = (v * lax.rsqrt(jnp.mean(v * v, axis=-1, keepdims=True) + EPS)).astype(BF16)
        mixed = jnp.dot(w_ref[g].astype(BF16), vn, preferred_element_type=F32) + b_ref[:, _head(g)]
        o_ref[:, _head(g)] = (u * mixed).astype(o_ref.dtype)


def spatial_gating(proj, sgu_w, sgu_b, layer):
    t = proj.shape[0]
    ch = sgu_w.shape[-1]
    depth = sgu_w.shape[0]
    bias = jnp.repeat(jnp.swapaxes(sgu_b, 1, 2), HEAD_DIM, axis=2)
    return pl.pallas_call(
        _sgu_kernel,
        grid=(t // ch,),
        in_specs=[
            pl.BlockSpec((ch, MIX_W), lambda i: (i, 3)),
            pl.BlockSpec((ch, MIX_W), lambda i: (i, 4)),
            pl.BlockSpec((None, N_HEADS, ch, ch), lambda i: (layer, 0, 0, 0)),
            pl.BlockSpec((None, ch, MIX_W), lambda i: (layer, 0, 0)),
        ],
        out_specs=pl.BlockSpec((ch, MIX_W), lambda i: (i, 0)),
        out_shape=jax.ShapeDtypeStruct((t, MIX_W), BF16),
        compiler_params=_params("arbitrary"),
        name="sgu",
    )(proj, proj, sgu_w, bias)


def _pool_kernel(x_ref, w_ref, s_ref, o_ref):
    n = x_ref.shape[0]
    t = lax.broadcasted_iota(jnp.int32, (n, n), 0)
    s = lax.broadcasted_iota(jnp.int32, (n, n), 1)
    tc = lax.broadcasted_iota(jnp.int32, (n, 1), 0)
    for g, win in enumerate(POOL_WINDOWS):
        half = win // 2
        band = jnp.where(s >= t - half, jnp.where(s < t + half, 1.0, 0.0), 0.0).astype(BF16)
        cnt = (jnp.minimum(tc + half, n) - jnp.maximum(tc - half, 0)).astype(F32)
        x = x_ref[:, _head(g)]
        hi = x.astype(BF16)
        lo = (x - hi.astype(F32)).astype(BF16)
        tot = jnp.dot(band, hi, preferred_element_type=F32) + jnp.dot(band, lo, preferred_element_type=F32)
        pooled = tot / cnt - x
        y = jnp.dot(pooled.astype(BF16), w_ref[g].astype(BF16), preferred_element_type=F32)
        o_ref[:, _head(g)] = (y * s_ref[:, _head(g)]).astype(o_ref.dtype)


def multiscale_pool(proj, pool_w, pool_scale, layer, row0, n_seq, seq_len):
    depth = pool_w.shape[0]
    return pl.pallas_call(
        _pool_kernel,
        grid=(n_seq,),
        in_specs=[
            pl.BlockSpec((seq_len, MIX_W), lambda b: (row0 // seq_len + b, N_PROJ - 1)),
            pl.BlockSpec((None, N_HEADS, HEAD_DIM, HEAD_DIM), lambda b: (layer, 0, 0, 0)),
            pl.BlockSpec((None, 1, MIX_W), lambda b: (layer, 0, 0)),
        ],
        out_specs=pl.BlockSpec((seq_len, MIX_W), lambda b: (b, 0)),
        out_shape=jax.ShapeDtypeStruct((n_seq * seq_len, MIX_W), BF16),
        compiler_params=_params("arbitrary"),
        name="pool",
    )(proj, pool_w, pool_scale.reshape(depth, 1, MIX_W))


HGRN_BLOCK = 128
HGRN_DIAG = 8


def _log_forget_k(z, lb):
    a = jnp.log(lb + LB_TINY)
    b = jnp.log1p(-lb) + (jnp.minimum(z, 0.0) - jnp.log1p(jnp.exp(-jnp.abs(z))))
    return jnp.maximum(a, b) + jnp.log1p(jnp.exp(-jnp.abs(a - b)))


def _rows_of(x, idx, span):
    return jnp.concatenate([jnp.broadcast_to(x[i:i + 1, :], (span, x.shape[1])) for i in idx], axis=0)


def _hgrn_block(q, k, v, logf, s0, rev):
    n = HGRN_BLOCK
    row = lax.broadcasted_iota(jnp.int32, (n, n), 0)
    col = lax.broadcasted_iota(jnp.int32, (n, n), 1)
    cum = logf
    sh = 1
    while sh < n:
        if rev:
            cum = cum + jnp.where(row < n - sh, pltpu.roll(cum, n - sh, 0), 0.0)
        else:
            cum = cum + jnp.where(row >= sh, pltpu.roll(cum, sh, 0), 0.0)
        sh *= 2
    att = jnp.zeros((n, n), F32)
    half = n // 2
    while half >= HGRN_DIAG:
        span = 2 * half
        pos = row % span
        first = (pos >= half) if rev else (pos < half)
        edge = half if rev else half - 1
        ref = _rows_of(cum, [j * span + edge for j in range(n // span)], span)
        qe = jnp.where(first, 0.0, q * jnp.exp(cum - ref))
        ke = jnp.where(first, k * jnp.exp(ref - cum), 0.0)
        a = lax.dot_general(qe.astype(BF16), ke.astype(BF16), _NT, preferred_element_type=F32)
        att = att + jnp.where(row // span == col // span, a, 0.0)
        half //= 2
    pos = row % HGRN_DIAG
    group_col = row - pos

    def row_of_group(x, j):
        x3 = x.reshape(n // HGRN_DIAG, HGRN_DIAG, n)
        return jnp.broadcast_to(x3[:, j:j + 1, :], x3.shape).reshape(n, n)

    for j in range(HGRN_DIAG):
        a = q * row_of_group(k, j) * jnp.exp(cum - row_of_group(cum, j))
        a = jnp.where((pos <= j) if rev else (pos >= j), a, 0.0)
        w = jnp.sum(a, axis=-1, keepdims=True)
        att = att + jnp.where(col == group_col + j, w, 0.0)
    o = jnp.dot(att.astype(BF16), v.astype(BF16), preferred_element_type=F32)
    o = o + jnp.dot((q * jnp.exp(cum)).astype(BF16), s0.astype(BF16), preferred_element_type=F32)
    last = cum[0:1, :] if rev else cum[n - 1:n, :]
    kd = (k * jnp.exp(last - cum)).astype(BF16)
    upd = lax.dot_general(kd, v.astype(BF16), _TN, preferred_element_type=F32)
    keep = jnp.sum(jnp.where(row == col, jnp.exp(last), 0.0), axis=-1, keepdims=True)
    return o, keep * s0 + upd


def _hgrn_kernel(qf_ref, if_ref, ff_ref, qb_ref, ib_ref, fb_ref, lbf_ref, lbb_ref, s0_ref,
                 of_ref, ob_ref, sfin_ref, state_ref, *, step_info):
    from_zero, c, nblk = step_info(pl.program_id(0))[:3]

    @pl.when((c == 0) & from_zero)
    def _():
        state_ref[...] = jnp.zeros_like(state_ref)

    @pl.when((c == 0) & jnp.logical_not(from_zero))
    def _():
        state_ref[...] = s0_ref[...]

    for rev, (q_ref, i_ref, f_ref, lb_ref, o_ref) in enumerate(
            ((qf_ref, if_ref, ff_ref, lbf_ref, of_ref), (qb_ref, ib_ref, fb_ref, lbb_ref, ob_ref))):
        for h in range(N_HEADS):
            q = q_ref[:, _head(h)]
            q = q * _sigmoid(q)
            logf = _log_forget_k(f_ref[:, _head(h)], lb_ref[:, _head(h)])
            k = 1.0 - jnp.exp(logf)
            o, s1 = _hgrn_block(q, k, i_ref[:, _head(h)], logf, state_ref[rev, h], bool(rev))
            o_ref[:, _head(h)] = o
            state_ref[rev, h] = s1

    @pl.when((c == nblk - 1) & from_zero)
    def _():
        sfin_ref[...] = state_ref[...]


def hgrn2_scan(proj, lb_f, lb_b, state_in, layer, n_zero, len_zero, n_init, len_init):
    blk = HGRN_BLOCK
    nz, ni = len_zero // blk, len_init // blk
    steps_zero = n_zero * nz

    def step_info(s):
        from_zero = s < steps_zero
        s2 = s - steps_zero
        b = jnp.where(from_zero, s // nz, s2 // ni)
        c = jnp.where(from_zero, s % nz, s2 % ni)
        nblk = jnp.where(from_zero, nz, ni)
        base = jnp.where(from_zero, b * nz, steps_zero + b * ni)
        return from_zero, c, nblk, b, base

    def fwd(group):
        def index(s):
            _, c, _, _, base = step_info(s)
            return (base + c, group)
        return pl.BlockSpec((blk, MIX_W), index)

    def bwd(group):
        def index(s):
            _, c, nblk, _, base = step_info(s)
            return (base + nblk - 1 - c, group)
        return pl.BlockSpec((blk, MIX_W), index)

    def init_index(s):
        from_zero, _, _, b, _ = step_info(s)
        return (jnp.where(from_zero, 0, b), layer, 0, 0, 0, 0)

    def final_index(s):
        from_zero, _, _, b, _ = step_info(s)
        return (jnp.where(from_zero, b, n_zero - 1), 0, 0, 0, 0)

    t = proj.shape[0]
    state_block = (None, 2, N_HEADS, HEAD_DIM, HEAD_DIM)
    lb_spec = pl.BlockSpec((1, MIX_W), lambda s: (0, 0))
    return pl.pallas_call(
        functools.partial(_hgrn_kernel, step_info=step_info),
        grid=(steps_zero + n_init * ni,),
        in_specs=[fwd(5), fwd(6), fwd(7), bwd(5), bwd(6), bwd(8), lb_spec, lb_spec,
                  pl.BlockSpec((None, None, 2, N_HEADS, HEAD_DIM, HEAD_DIM), init_index)],
        out_specs=[fwd(0), bwd(0), pl.BlockSpec(state_block, final_index)],
        out_shape=[jax.ShapeDtypeStruct((t, MIX_W), F32), jax.ShapeDtypeStruct((t, MIX_W), F32),
                   jax.ShapeDtypeStruct((n_zero, 2, N_HEADS, HEAD_DIM, HEAD_DIM), F32)],
        scratch_shapes=[pltpu.VMEM((2, N_HEADS, HEAD_DIM, HEAD_DIM), F32)],
        compiler_params=_params("arbitrary"),
        name="hgrn_scan",
    )(proj, proj, proj, proj, proj, proj, lb_f.reshape(1, MIX_W), lb_b.reshape(1, MIX_W), state_in)


def _hgrn_out_kernel(of_ref, ob_ref, g_ref, ng_ref, o_ref):
    for h in range(N_HEADS):
        o = of_ref[:, _head(h)] + ob_ref[:, _head(h)]
        o = o * lax.rsqrt(jnp.mean(o * o, axis=-1, keepdims=True) + EPS)
        g = g_ref[:, _head(h)]
        o_ref[:, _head(h)] = (o * ng_ref[:, _head(h)] * (g * _sigmoid(g))).astype(o_ref.dtype)


def hgrn2_output(o_f, o_b, proj, norm_g, layer, tm=512):
    t = o_f.shape[0]
    depth = norm_g.shape[0]
    tile = pl.BlockSpec((tm, MIX_W), lambda i: (i, 0))
    return pl.pallas_call(
        _hgrn_out_kernel,
        grid=(t // tm,),
        in_specs=[tile, tile, pl.BlockSpec((tm, MIX_W), lambda i: (i, 9)),
                  pl.BlockSpec((None, 1, MIX_W), lambda i: (layer, 0, 0))],
        out_specs=tile,
        out_shape=jax.ShapeDtypeStruct((t, MIX_W), BF16),
        compiler_params=_params("arbitrary"),
        name="hgrn_out",
    )(o_f, o_b, proj, norm_g.reshape(depth, 1, MIX_W))


def kernel(x_prompt, x_sample, cache_k, cache_v, state_hgrn, c, c_ctx, w_mod, b_mod, norm1_g, norm2_g, w_in, na_rpb, sgu_w, sgu_b, hgrn_lb, hgrn_norm_g, pool_w, pool_scale, w_out, router_w, router_b, exp_w_gu, exp_b_gu, exp_w_dn, exp_b_dn, final_norm_g):
    bp, sp, d = x_prompt.shape
    bs, ss, _ = x_sample.shape
    depth = w_mod.shape[0]
    tp, ts = bp * sp, bs * ss

    cond = jnp.zeros((8, d), F32).at[0].set(c_ctx).at[1:1 + bs].set(c)
    mods = adaln_all(cond, w_mod, b_mod).reshape(depth * 8 * N_MOD, 1, d)

    def mod_of_tile(tile_rows):
        prompt_tiles = tp // tile_rows
        per_seq = ss // tile_rows
        return lambda i: jnp.where(i < prompt_tiles, 0, 1 + (i - prompt_tiles) // per_seq)

    lb_p = jax.nn.softmax(hgrn_lb, axis=1)
    lb_all = jnp.maximum(jnp.cumsum(lb_p, axis=1) - lb_p[:, :1], 0.0)

    x = jnp.concatenate([x_prompt.reshape(tp, d), x_sample.reshape(ts, d)], axis=0)
    g1 = norm1_g.reshape(depth, 1, d)
    g2 = norm2_g.reshape(depth, 1, d)
    bias_table = na_bias_table(na_rpb)
    cache_k = cache_k.reshape(bs, depth, cache_k.shape[2], MIX_W)
    cache_v = cache_v.reshape(bs, depth, cache_v.shape[2], MIX_W)
    new_k, new_v, new_s = [], [], []
    h = norm_mod(x, g1, mods, 0, 1, 0, mod_of_tile(ROW_TILE))
    for l in range(depth):
        last = l == depth - 1
        proj = proj_in(h, w_in, l)
        o_a = jnp.concatenate([context_attention(proj, bp, sp),
                               neighbourhood_attention(proj, cache_k, cache_v, bias_table, l, tp, bs, ss)])
        o_b = spatial_gating(proj, sgu_w, sgu_b, l)
        o_f, o_r, s_l = hgrn2_scan(proj, lb_all[0, l], lb_all[1, l], state_hgrn, l, bp, sp, bs, ss)
        o_c = hgrn2_output(o_f, o_r, proj, hgrn_norm_g, l)
        o_d = jnp.concatenate([multiscale_pool(proj, pool_w, pool_scale, l, 0, bp, sp),
                               multiscale_pool(proj, pool_w, pool_scale, l, tp, bs, ss)])
        new_k.append(proj[:tp, MIX_W:2 * MIX_W].reshape(bp, sp, N_HEADS, HEAD_DIM))
        new_v.append(proj[:tp, 2 * MIX_W:3 * MIX_W].reshape(bp, sp, N_HEADS, HEAD_DIM))
        new_s.append(s_l)
        x = proj_out_residual([o_a, o_b, o_c, o_d], w_out, x, mods, l, mod_of_tile(1024))
        x, h = moe_layer(x, g2, mods, router_w, router_b, exp_w_gu, exp_b_gu, exp_w_dn, exp_b_dn, l,
                         mod_of_tile(ROW_TILE), final_norm_g.reshape(1, 1, d) if last else g1,
                         tp if last else None)
    return (x.reshape(bp, sp, d), h.reshape(bs, ss, d),
            jnp.stack(new_k, axis=1), jnp.stack(new_v, axis=1), jnp.stack(new_s, axis=1))
```
